```python
import jax, jax.numpy as jnp
from jax import lax
import numpy as np

D_MODEL = 2048
BATCH = 32
SEQ = 256
DEPTH = 1
DEC_BATCH = 8
DEC_SEQ = 1024
PAST_LEN = 256

GRID_W = 64
HEAD_DIM = 128
N_HEADS_TOTAL = D_MODEL // HEAD_DIM
N_HEADS_A = N_HEADS_TOTAL // 2
N_HEADS_B = N_HEADS_TOTAL - N_HEADS_A
N_KV_B = max(1, N_HEADS_B // 4)
MIX_WIDTH = N_HEADS_TOTAL * HEAD_DIM
WIN_H = 8
WIN_W = 16
D_FF = 5632
ROPE_THETA = 10000.0
EPS = 1e-6
Q_BLOCK = 128
NEG_INF = -1e30
A_WIDTH = N_HEADS_A * HEAD_DIM
B_Q_WIDTH = N_HEADS_B * HEAD_DIM
B_KV_WIDTH = N_KV_B * HEAD_DIM
IN_COLS = 3 * A_WIDTH + B_Q_WIDTH + 2 * B_KV_WIDTH

kernel_name = "hybrid_natten_gqa_dit_step"


def rms_norm(x, g):
    xf = x.astype(jnp.float32)
    y = xf * lax.rsqrt(jnp.mean(xf * xf, axis=-1, keepdims=True) + EPS)
    return (y * g.astype(jnp.float32)).astype(x.dtype)


def modulation(cond, w_mod, b_mod):
    m = jax.nn.silu(cond) @ w_mod + b_mod
    return jnp.split(m, 6, axis=-1)


def split_heads(x, n):
    b, l, _ = x.shape
    return x.reshape(b, l, n, HEAD_DIM).transpose(0, 2, 1, 3)


def merge_heads(x):
    b, n, l, d = x.shape
    return x.transpose(0, 2, 1, 3).reshape(b, l, n * d)


def project(h, w_in):
    qkv = h @ w_in
    qa, ka, va, qb, kb, vb = jnp.split(
        qkv, [A_WIDTH, 2 * A_WIDTH, 3 * A_WIDTH, 3 * A_WIDTH + B_Q_WIDTH,
              3 * A_WIDTH + B_Q_WIDTH + B_KV_WIDTH], axis=-1)
    return (split_heads(qa, N_HEADS_A), split_heads(ka, N_HEADS_A), split_heads(va, N_HEADS_A),
            split_heads(qb, N_HEADS_B), split_heads(kb, N_KV_B), split_heads(vb, N_KV_B))


def rope_2d(x):
    l = x.shape[2]
    t = jnp.arange(l)
    half = HEAD_DIM // 2
    freqs = ROPE_THETA ** (-jnp.arange(0, half, 2, dtype=jnp.float32) / half)
    xf = x.astype(jnp.float32)

    def rot(xh, pos):
        ang = pos.astype(jnp.float32)[:, None] * freqs[None, :]
        cos, sin = jnp.cos(ang), jnp.sin(ang)
        x1, x2 = xh[..., :half // 2], xh[..., half // 2:]
        return jnp.concatenate([x1 * cos - x2 * sin, x2 * cos + x1 * sin], axis=-1)

    out = jnp.concatenate([rot(xf[..., :half], t // GRID_W), rot(xf[..., half:], t % GRID_W)], axis=-1)
    return out.astype(x.dtype)


def attend(q, k, v):
    b, hq, lq, dh = q.shape
    hkv = k.shape[1]
    g = hq // hkv
    nblk = lq // Q_BLOCK
    scale = dh ** -0.5
    qb = q.reshape(b, hkv, g, nblk, Q_BLOCK, dh).transpose(3, 0, 1, 2, 4, 5)

    def block(qi):
        s = jnp.einsum('bkgqd,bkld->bkgql', qi, k).astype(jnp.float32) * scale
        p = jax.nn.softmax(s, axis=-1).astype(v.dtype)
        return jnp.einsum('bkgql,bkld->bkgqd', p, v)

    o = lax.map(block, qb)
    return o.transpose(1, 2, 3, 0, 4, 5).reshape(b, hq, lq, dh)


def neighbourhood_attention(q, k, v, k_ctx, v_ctx, rpb):
    b, h, l, dh = q.shape
    rows = l // GRID_W
    kh = min(WIN_H, rows)
    scale = dh ** -0.5
    r = jnp.arange(rows)
    rs = jnp.clip(r - kh // 2, 0, rows - kh)
    row_idx = rs[:, None] + jnp.arange(kh)[None, :]
    col = jnp.arange(GRID_W)
    cs = jnp.clip(col - WIN_W // 2, 0, GRID_W - WIN_W)
    col_valid = (col[None, :] >= cs[:, None]) & (col[None, :] < cs[:, None] + WIN_W)
    qg = q.reshape(b, h, rows, GRID_W, dh)
    kg = k.reshape(b, h, rows, GRID_W, dh)
    vg = v.reshape(b, h, rows, GRID_W, dh)
    k_rows = kg[:, :, row_idx]
    v_rows = vg[:, :, row_idx]
    roff = row_idx - r[:, None] + WIN_H - 1
    coff = jnp.clip(col[None, :] - col[:, None] + WIN_W - 1, 0, 2 * WIN_W - 2)
    bias = rpb[:, roff[:, None, :, None], coff[None, :, None, :]]
    s_nb = jnp.einsum('bhrqd,bhrikd->bhrqik', qg, k_rows).astype(jnp.float32) * scale
    s_nb = s_nb + bias.astype(jnp.float32)
    s_nb = jnp.where(col_valid[:, None, :], s_nb, NEG_INF)
    s_nb = s_nb.reshape(b, h, rows, GRID_W, kh * GRID_W)
    s_ctx = jnp.einsum('bhrqd,bhcd->bhrqc', qg, k_ctx).astype(jnp.float32) * scale
    p = jax.nn.softmax(jnp.concatenate([s_nb, s_ctx], axis=-1), axis=-1).astype(v.dtype)
    p_nb = p[..., :kh * GRID_W].reshape(b, h, rows, GRID_W, kh, GRID_W)
    p_ctx = p[..., kh * GRID_W:]
    o = (jnp.einsum('bhrqik,bhrikd->bhrqd', p_nb, v_rows)
         + jnp.einsum('bhrqc,bhcd->bhrqd', p_ctx, v_ctx))
    return o.reshape(b, h, l, dh)


def conv_ffn(h, w_up, conv_w, conv_b, w_down):
    u = h @ w_up
    up = jnp.pad(u, ((0, 0), (1, 1), (0, 0)))
    u = up[:, :-2] * conv_w[0] + up[:, 1:-1] * conv_w[1] + up[:, 2:] * conv_w[2] + conv_b
    val, gate = jnp.split(u, 2, axis=-1)
    return (jax.nn.silu(gate) * val) @ w_down


def setup_inputs(seed: int = 0) -> dict:
    key = jax.random.key(seed)
    ks = jax.random.split(key, 24)
    f32 = jnp.float32
    nrm = lambda k, s, sc: jax.random.normal(k, s, f32) * sc
    return {
        "x_prompt": nrm(ks[0], (BATCH, SEQ, D_MODEL), 1.0),
        "x_sample": nrm(ks[1], (DEC_BATCH, DEC_SEQ, D_MODEL), 1.0),
        "c": nrm(ks[2], (DEC_BATCH, D_MODEL), 1.0),
        "cache_a_k": nrm(ks[3], (DEC_BATCH, DEPTH, N_HEADS_A, PAST_LEN, HEAD_DIM), 1.0),
        "cache_a_v": nrm(ks[4], (DEC_BATCH, DEPTH, N_HEADS_A, PAST_LEN, HEAD_DIM), 1.0),
        "cache_b_k": nrm(ks[5], (DEC_BATCH, DEPTH, N_KV_B, PAST_LEN, HEAD_DIM), 1.0),
        "cache_b_v": nrm(ks[6], (DEC_BATCH, DEPTH, N_KV_B, PAST_LEN, HEAD_DIM), 1.0),
        "c_ctx": nrm(ks[7], (D_MODEL,), 1.0),
        "w_mod": nrm(ks[8], (DEPTH, D_MODEL, 6 * D_MODEL), D_MODEL ** -0.5),
        "b_mod": nrm(ks[9], (DEPTH, 6 * D_MODEL), 0.01),
        "g_attn_pre": 1.0 + nrm(ks[10], (DEPTH, D_MODEL), 0.01),
        "g_attn_post": 1.0 + nrm(ks[11], (DEPTH, D_MODEL), 0.01),
        "g_ffn_pre": 1.0 + nrm(ks[12], (DEPTH, D_MODEL), 0.01),
        "g_ffn_post": 1.0 + nrm(ks[13], (DEPTH, D_MODEL), 0.01),
        "w_in": nrm(ks[14], (DEPTH, D_MODEL, IN_COLS), D_MODEL ** -0.5),
        "rpb": nrm(ks[15], (DEPTH, N_HEADS_A, 2 * WIN_H - 1, 2 * WIN_W - 1), 0.1),
        "g_qnorm": 1.0 + nrm(ks[16], (DEPTH, HEAD_DIM), 0.01),
        "g_knorm": 1.0 + nrm(ks[17], (DEPTH, HEAD_DIM), 0.01),
        "w_out": nrm(ks[18], (DEPTH, MIX_WIDTH, D_MODEL), MIX_WIDTH ** -0.5),
        "w_up": nrm(ks[19], (DEPTH, D_MODEL, 2 * D_FF), D_MODEL ** -0.5),
        "conv_w": nrm(ks[20], (DEPTH, 3, 2 * D_FF), 3 ** -0.5),
        "conv_b": nrm(ks[21], (DEPTH, 2 * D_FF), 0.01),
        "w_down": nrm(ks[22], (DEPTH, D_FF, D_MODEL), D_FF ** -0.5),
    }


def reference(x_prompt, x_sample, c, cache_a_k, cache_a_v, cache_b_k, cache_b_v, c_ctx,
              w_mod, b_mod, g_attn_pre, g_attn_post, g_ffn_pre, g_ffn_post, w_in, rpb,
              g_qnorm, g_knorm, w_out, w_up, conv_w, conv_b, w_down):
    xp = x_prompt
    xs = x_sample
    st_ak, st_av, st_bk, st_bv = [], [], [], []
    for l in range(DEPTH):
        sh1, sc1, ga1, sh2, sc2, ga2 = modulation(c_ctx[None, :], w_mod[l], b_mod[l])
        h = rms_norm(xp, g_attn_pre[l]) * (1 + sc1) + sh1
        qa, ka, va, qb, kb, vb = project(h, w_in[l])
        qb = rms_norm(qb, g_qnorm[l])
        kb = rms_norm(kb, g_knorm[l])
        oa = attend(qa, ka, va)
        ob = attend(qb, kb, vb)
        o = jnp.concatenate([merge_heads(oa), merge_heads(ob)], axis=-1) @ w_out[l]
        xp = xp + ga1 * rms_norm(o, g_attn_post[l])
        h = rms_norm(xp, g_ffn_pre[l]) * (1 + sc2) + sh2
        f = conv_ffn(h, w_up[l], conv_w[l], conv_b[l], w_down[l])
        xp = xp + ga2 * rms_norm(f, g_ffn_post[l])
        st_ak.append(ka)
        st_av.append(va)
        st_bk.append(kb)
        st_bv.append(vb)

        sh1, sc1, ga1, sh2, sc2, ga2 = modulation(c[:, None, :], w_mod[l], b_mod[l])
        h = rms_norm(xs, g_attn_pre[l]) * (1 + sc1) + sh1
        qa, ka, va, qb, kb, vb = project(h, w_in[l])
        oa = neighbourhood_attention(qa, ka, va, cache_a_k[:, l], cache_a_v[:, l], rpb[l])
        qb = rope_2d(rms_norm(qb, g_qnorm[l]))
        kb = rope_2d(rms_norm(kb, g_knorm[l]))
        kb_all = jnp.concatenate([cache_b_k[:, l], kb], axis=2)
        vb_all = jnp.concatenate([cache_b_v[:, l], vb], axis=2)
        ob = attend(qb, kb_all, vb_all)
        o = jnp.concatenate([merge_heads(oa), merge_heads(ob)], axis=-1) @ w_out[l]
        xs = xs + ga1 * rms_norm(o, g_attn_post[l])
        h = rms_norm(xs, g_ffn_pre[l]) * (1 + sc2) + sh2
        f = conv_ffn(h, w_up[l], conv_w[l], conv_b[l], w_down[l])
        xs = xs + ga2 * rms_norm(f, g_ffn_post[l])

    state_a_k = jnp.stack(st_ak, axis=1)
    state_a_v = jnp.stack(st_av, axis=1)
    state_b_k = jnp.stack(st_bk, axis=1)
    state_b_v = jnp.stack(st_bv, axis=1)
    return (xp, xs, state_a_k, state_a_v, state_b_k, state_b_v)
```

```python
import functools

import numpy as np
import jax
import jax.numpy as jnp
from jax import lax
from jax.experimental import pallas as pl
from jax.experimental.pallas import tpu as pltpu

F32 = jnp.float32
BF16 = jnp.bfloat16

D_MODEL = 2048
HEAD_DIM = 128
N_HEADS_A = 8
N_HEADS_B = 8
N_KV_B = 2
GRID_W = 64
GRID_H = 16
WIN_H = 8
WIN_W = 16
D_FF = 5632
ROPE_THETA = 10000.0
EPS = 1e-6
NEG_INF = -1e30
SCALE = HEAD_DIM ** -0.5
IN_COLS = 4608
N_SLOTS = IN_COLS // HEAD_DIM
QKV_TN = 512
N_QKV_TILES = IN_COLS // QKV_TN
FFN_TF = 512
N_FFN_TILES = D_FF // FFN_TF
HALF_Q = 512
BAND_K = 768
VMEM_LIMIT = 56 * 1024 * 1024


def _cparams(sem):
    return pltpu.CompilerParams(dimension_semantics=sem, vmem_limit_bytes=VMEM_LIMIT)


def _rms(x):
    return x * lax.rsqrt(jnp.mean(x * x, axis=-1, keepdims=True) + EPS)


def _mod_kernel(c_ref, w_ref, b_ref, o_ref):
    c = c_ref[...]
    a = (c / (1.0 + jnp.exp(-c))).astype(BF16)
    o_ref[...] = jnp.dot(a, w_ref[...].astype(BF16), preferred_element_type=F32) + b_ref[...]


def _modulation(conds, w_mod, b_mod):
    tn = 1024
    n = w_mod.shape[1]
    return pl.pallas_call(
        _mod_kernel,
        grid=(n // tn,),
        in_specs=[
            pl.BlockSpec((16, D_MODEL), lambda j: (0, 0)),
            pl.BlockSpec((D_MODEL, tn), lambda j: (0, j)),
            pl.BlockSpec((1, tn), lambda j: (0, j)),
        ],
        out_specs=pl.BlockSpec((16, tn), lambda j: (0, j)),
        out_shape=jax.ShapeDtypeStruct((16, n), F32),
        compiler_params=_cparams(("arbitrary",)),
        name="modulation",
    )(conds, w_mod, b_mod.reshape(1, n))


def _bias_kernel(rpb_ref, o_ref):
    h = pl.program_id(0)
    qc = lax.broadcasted_iota(jnp.int32, (GRID_W, 128), 0)
    lane = lax.broadcasted_iota(jnp.int32, (GRID_W, 128), 1)
    kc = lane & (GRID_W - 1)
    dmat = kc - qc + (WIN_W - 1)
    cs = jnp.clip(qc - WIN_W // 2, 0, GRID_W - WIN_W)
    col_valid = (kc >= cs) & (kc < cs + WIN_W)
    neg = jnp.full((GRID_W, 128), NEG_INF, F32)

    g = []
    for dr in range(2 * WIN_H - 1):
        acc = neg
        for d in range(2 * WIN_W - 1):
            acc = jnp.where(dmat == d, rpb_ref[h, dr * (2 * WIN_W - 1) + d], acc)
        g.append(jnp.where(col_valid, acc, neg))

    def row_tile(r, kr):
        rs = min(max(r - WIN_H // 2, 0), GRID_H - WIN_H)
        if rs <= kr < rs + WIN_H:
            return g[kr - r + WIN_H - 1]
        return neg

    for half in range(2):
        for lr in range(HALF_Q // GRID_W):
            r = 8 * half + lr
            for p in range(BAND_K // 128):
                kr0 = 4 * half + 2 * p
                tile = jnp.where(lane < GRID_W, row_tile(r, kr0), row_tile(r, kr0 + 1))
                o_ref[0, half, lr * GRID_W:(lr + 1) * GRID_W, p * 128:(p + 1) * 128] = tile


def _bias_table(rpb_l):
    rpb2 = rpb_l.reshape(N_HEADS_A, (2 * WIN_H - 1) * (2 * WIN_W - 1))
    return pl.pallas_call(
        _bias_kernel,
        grid=(N_HEADS_A,),
        in_specs=[pl.BlockSpec(memory_space=pltpu.SMEM)],
        out_specs=pl.BlockSpec((1, 2, HALF_Q, BAND_K), lambda h: (h, 0, 0, 0)),
        out_shape=jax.ShapeDtypeStruct((N_HEADS_A, 2, HALF_Q, BAND_K), F32),
        compiler_params=_cparams(("arbitrary",)),
        name="bias_table",
    )(rpb2)


def _rope(y, cos, sin_signed):
    lane = lax.broadcasted_iota(jnp.int32, y.shape, 1)
    swapped = jnp.where((lane & 63) < 32, pltpu.roll(y, 96, 1), pltpu.roll(y, 32, 1))
    return y * cos + swapped * sin_signed


def _qkv_kernel(*refs, is_ctx, tm, seq):
    if is_ctx:
        (x_ref, mod_ref, gpre_ref, w_ref, gq_ref, gk_ref,
         out_ref, sak_ref, sav_ref, sbk_ref, sbv_ref, h_scr) = refs
    else:
        (x_ref, mod_ref, gpre_ref, w_ref, gq_ref, gk_ref, cos_ref, sin_ref,
         out_ref, h_scr) = refs
    n = pl.program_id(1)
    nb = tm // seq
    chunk = 256

    @pl.when(n == 0)
    def _():
        sh1 = mod_ref[0, 0:1, :]
        sc1 = mod_ref[0, 1:2, :]
        gpre = gpre_ref[...]

        def body(c, carry):
            rows = pl.ds(pl.multiple_of(c * chunk, chunk), chunk)
            y = _rms(x_ref[rows, :]) * gpre
            h_scr[rows, :] = (y * (1.0 + sc1) + sh1).astype(BF16)
            return carry

        lax.fori_loop(0, tm // chunk, body, 0)

    res = jnp.dot(h_scr[...], w_ref[...], preferred_element_type=F32)

    def head(hh):
        return res[:, hh * HEAD_DIM:(hh + 1) * HEAD_DIM]

    def store_state(s_ref, hh_out, val):
        for bb in range(nb):
            s_ref[bb, 0, hh_out] = val[bb * seq:(bb + 1) * seq]

    def rope(y):
        if is_ctx:
            return y
        return _rope(y, cos_ref[...], sin_ref[...])

    @pl.when(n < 2)
    def _():
        for hh in range(4):
            out_ref[hh] = (head(hh) * SCALE).astype(BF16)

    @pl.when((n >= 2) & (n < 6))
    def _():
        for hh in range(4):
            out_ref[hh] = head(hh).astype(BF16)

    if is_ctx:
        @pl.when((n >= 2) & (n < 4))
        def _():
            for hh in range(4):
                store_state(sak_ref, hh, head(hh))

        @pl.when((n >= 4) & (n < 6))
        def _():
            for hh in range(4):
                store_state(sav_ref, hh, head(hh))

    @pl.when((n >= 6) & (n < 8))
    def _():
        gq = gq_ref[...]
        for hh in range(4):
            y = _rms(head(hh)) * gq
            out_ref[hh] = (rope(y) * SCALE).astype(BF16)

    @pl.when(n == 8)
    def _():
        gk = gk_ref[...]
        for hh in range(2):
            y = _rms(head(hh)) * gk
            if is_ctx:
                store_state(sbk_ref, hh, y)
            out_ref[hh] = rope(y).astype(BF16)
        for hh in range(2, 4):
            v = head(hh)
            if is_ctx:
                store_state(sbv_ref, hh - 2, v)
            out_ref[hh] = v.astype(BF16)


def _qkv(x2d, mods, g_pre, w_in_bf, g_q, g_k, *, is_ctx, seq, rope_tabs=None):
    t = x2d.shape[0]
    tm = 1024
    nb = tm // seq
    n_tiles = t // tm
    nmod = mods.shape[0]
    mod_idx = (lambda i, n: (0, 0, 0)) if nmod == 1 else (lambda i, n: (i, 0, 0))
    in_specs = [
        pl.BlockSpec((tm, D_MODEL), lambda i, n: (i, 0)),
        pl.BlockSpec((1, 6, D_MODEL), mod_idx),
        pl.BlockSpec((1, D_MODEL), lambda i, n: (0, 0)),
        pl.BlockSpec((D_MODEL, QKV_TN), lambda i, n: (0, n)),
        pl.BlockSpec((1, HEAD_DIM), lambda i, n: (0, 0)),
        pl.BlockSpec((1, HEAD_DIM), lambda i, n: (0, 0)),
    ]
    args = [x2d, mods, g_pre, w_in_bf, g_q, g_k]
    out_specs = [pl.BlockSpec((4, tm, HEAD_DIM), lambda i, n: (n, i, 0))]
    out_shape = [jax.ShapeDtypeStruct((N_SLOTS, t, HEAD_DIM), BF16)]
    if is_ctx:
        nbatch = t // seq
        out_specs += [
            pl.BlockSpec((nb, 1, 4, seq, HEAD_DIM), lambda i, n: (i, 0, jnp.clip(n - 2, 0, 1), 0, 0)),
            pl.BlockSpec((nb, 1, 4, seq, HEAD_DIM), lambda i, n: (i, 0, jnp.clip(n - 4, 0, 1), 0, 0)),
            pl.BlockSpec((nb, 1, 2, seq, HEAD_DIM), lambda i, n: (i, 0, 0, 0, 0)),
            pl.BlockSpec((nb, 1, 2, seq, HEAD_DIM), lambda i, n: (i, 0, 0, 0, 0)),
        ]
        out_shape += [
            jax.ShapeDtypeStruct((nbatch, 1, N_HEADS_A, seq, HEAD_DIM), F32),
            jax.ShapeDtypeStruct((nbatch, 1, N_HEADS_A, seq, HEAD_DIM), F32),
            jax.ShapeDtypeStruct((nbatch, 1, N_KV_B, seq, HEAD_DIM), F32),
            jax.ShapeDtypeStruct((nbatch, 1, N_KV_B, seq, HEAD_DIM), F32),
        ]
    else:
        in_specs += [
            pl.BlockSpec((tm, HEAD_DIM), lambda i, n: (0, 0)),
            pl.BlockSpec((tm, HEAD_DIM), lambda i, n: (0, 0)),
        ]
        args += list(rope_tabs)
    return pl.pallas_call(
        functools.partial(_qkv_kernel, is_ctx=is_ctx, tm=tm, seq=seq),
        grid=(n_tiles, N_QKV_TILES),
        in_specs=in_specs,
        out_specs=out_specs,
        out_shape=out_shape,
        scratch_shapes=[pltpu.VMEM((tm, D_MODEL), BF16)],
        compiler_params=_cparams(("arbitrary", "arbitrary")),
        name="qkv_ctx" if is_ctx else "qkv_lat",
    )(*args)


def _qk(q, k):
    return lax.dot_general(q, k, (((1,), (1,)), ((), ())), preferred_element_type=F32)


def _softmax_pv(scores, values):
    m = functools.reduce(jnp.maximum, [jnp.max(s, axis=-1, keepdims=True) for s in scores])
    acc = None
    den = None
    for s, v in zip(scores, values):
        e = jnp.exp(s - m)
        l = jnp.sum(e, axis=-1, keepdims=True)
        o = jnp.dot(e.astype(BF16), v, preferred_element_type=F32)
        acc = o if acc is None else acc + o
        den = l if den is None else den + l
    return acc / den


def _attn_ctx_kernel(qkv_ref, o_ref, *, nb, seq):
    for bb in range(nb):
        rows = slice(bb * seq, (bb + 1) * seq)
        for h in range(N_HEADS_A):
            q = qkv_ref[h, rows, :]
            k = qkv_ref[N_HEADS_A + h, rows, :]
            v = qkv_ref[2 * N_HEADS_A + h, rows, :]
            o = _softmax_pv([_qk(q, k)], [v])
            o_ref[rows, h * HEAD_DIM:(h + 1) * HEAD_DIM] = o.astype(BF16)
        group = N_HEADS_B // N_KV_B
        for j in range(N_KV_B):
            k = qkv_ref[32 + j, rows, :]
            v = qkv_ref[34 + j, rows, :]
            for g in range(group):
                hq = j * group + g
                q = qkv_ref[24 + hq, rows, :]
                o = _softmax_pv([_qk(q, k)], [v])
                c0 = (N_HEADS_A + hq) * HEAD_DIM
                o_ref[rows, c0:c0 + HEAD_DIM] = o.astype(BF16)


def _attn_ctx(qkvh, *, seq):
    t = qkvh.shape[1]
    nb = 2
    rows = nb * seq
    return pl.pallas_call(
        functools.partial(_attn_ctx_kernel, nb=nb, seq=seq),
        grid=(t // rows,),
        in_specs=[pl.BlockSpec((N_SLOTS, rows, HEAD_DIM), lambda i: (0, i, 0))],
        out_specs=pl.BlockSpec((rows, D_MODEL), lambda i: (i, 0)),
        out_shape=jax.ShapeDtypeStruct((t, D_MODEL), BF16),
        compiler_params=_cparams(("arbitrary",)),
        name="attn_ctx",
    )(qkvh)


def _attn_lat_a_kernel(q_ref, k0_ref, k1_ref, k2_ref, v0_ref, v1_ref, v2_ref,
                       ck_ref, cv_ref, bias_ref, o_ref):
    for h in range(N_HEADS_A):
        q = q_ref[h]
        s_nb = jnp.concatenate([_qk(q, k0_ref[h]), _qk(q, k1_ref[h]), _qk(q, k2_ref[h])], axis=1)
        s_nb = s_nb + bias_ref[h, 0]
        s_cx = _qk(q, ck_ref[0, 0, h].astype(BF16))
        v_nb = jnp.concatenate([v0_ref[h], v1_ref[h], v2_ref[h]], axis=0)
        o = _softmax_pv([s_nb, s_cx], [v_nb, cv_ref[0, 0, h].astype(BF16)])
        o_ref[:, h * HEAD_DIM:(h + 1) * HEAD_DIM] = o.astype(BF16)


def _attn_lat_a(qkvh, cache_k, cache_v, bias):
    t = qkvh.shape[1]
    nbatch = t // 1024
    kq = 256

    def kspec(slot_blk, j):
        return pl.BlockSpec((8, kq, HEAD_DIM), lambda hf, b: (slot_blk, b * 4 + hf + j, 0))

    cache_spec = pl.BlockSpec((1, 1, N_HEADS_A, 256, HEAD_DIM), lambda hf, b: (b, 0, 0, 0, 0))
    return pl.pallas_call(
        _attn_lat_a_kernel,
        grid=(2, nbatch),
        in_specs=[
            pl.BlockSpec((8, HALF_Q, HEAD_DIM), lambda hf, b: (0, b * 2 + hf, 0)),
            kspec(1, 0), kspec(1, 1), kspec(1, 2),
            kspec(2, 0), kspec(2, 1), kspec(2, 2),
            cache_spec, cache_spec,
            pl.BlockSpec((N_HEADS_A, 1, HALF_Q, BAND_K), lambda hf, b: (0, hf, 0, 0)),
        ],
        out_specs=pl.BlockSpec((HALF_Q, N_HEADS_A * HEAD_DIM), lambda hf, b: (b * 2 + hf, 0)),
        out_shape=jax.ShapeDtypeStruct((t, N_HEADS_A * HEAD_DIM), BF16),
        compiler_params=_cparams(("arbitrary", "arbitrary")),
        name="attn_lat_a",
    )(qkvh, qkvh, qkvh, qkvh, qkvh, qkvh, qkvh, cache_k, cache_v, bias)


def _attn_lat_b_kernel(q_ref, k_ref, v_ref, ck_ref, cv_ref, o_ref):
    k = k_ref[0]
    v = v_ref[0]
    ck = ck_ref[0, 0, 0].astype(BF16)
    cv = cv_ref[0, 0, 0].astype(BF16)
    for g in range(N_HEADS_B // N_KV_B):
        for qh in range(2):
            rows = slice(qh * 512, (qh + 1) * 512)
            q = q_ref[g, rows, :]
            o = _softmax_pv([_qk(q, k), _qk(q, ck)], [v, cv])
            o_ref[rows, g * HEAD_DIM:(g + 1) * HEAD_DIM] = o.astype(BF16)


def _attn_lat_b(qkvh, cache_k, cache_v):
    t = qkvh.shape[1]
    nbatch = t // 1024
    cache_spec = pl.BlockSpec((1, 1, 1, 256, HEAD_DIM), lambda b, j: (b, 0, j, 0, 0))
    return pl.pallas_call(
        _attn_lat_b_kernel,
        grid=(nbatch, N_KV_B),
        in_specs=[
            pl.BlockSpec((4, 1024, HEAD_DIM), lambda b, j: (6 + j, b, 0)),
            pl.BlockSpec((1, 1024, HEAD_DIM), lambda b, j: (32 + j, b, 0)),
            pl.BlockSpec((1, 1024, HEAD_DIM), lambda b, j: (34 + j, b, 0)),
            cache_spec, cache_spec,
        ],
        out_specs=pl.BlockSpec((1024, 512), lambda b, j: (b, j)),
        out_shape=jax.ShapeDtypeStruct((t, N_HEADS_B * HEAD_DIM), BF16),
        compiler_params=_cparams(("arbitrary", "arbitrary")),
        name="attn_lat_b",
    )(qkvh, qkvh, qkvh, cache_k, cache_v)


def _proj_kernel(oa_ref, ob_ref, x_ref, mod_ref, w_ref, gpost_ref, gffn_ref, x1_ref, h2_ref):
    ga1 = mod_ref[0, 2:3, :]
    sh2 = mod_ref[0, 3:4, :]
    sc2 = mod_ref[0, 4:5, :]
    ka = oa_ref.shape[1]
    t = (jnp.dot(oa_ref[...], w_ref[:ka, :], preferred_element_type=F32)
         + jnp.dot(ob_ref[...], w_ref[ka:, :], preferred_element_type=F32))
    x1 = x_ref[...] + ga1 * (_rms(t) * gpost_ref[...])
    x1_ref[...] = x1
    h2_ref[...] = (_rms(x1) * gffn_ref[...] * (1.0 + sc2) + sh2).astype(BF16)


def _proj(o_a, o_b, x2d, mods, w_out_bf, g_post, g_ffn, *, tiles_per_mod):
    t = x2d.shape[0]
    tm = 512
    nmod = mods.shape[0]
    mod_idx = (lambda i: (0, 0, 0)) if nmod == 1 else (lambda i: (i // tiles_per_mod, 0, 0))
    ka = D_MODEL // 2
    ob_idx = (lambda i: (i, 1)) if o_b.shape[1] == D_MODEL else (lambda i: (i, 0))
    return pl.pallas_call(
        _proj_kernel,
        grid=(t // tm,),
        in_specs=[
            pl.BlockSpec((tm, ka), lambda i: (i, 0)),
            pl.BlockSpec((tm, ka), ob_idx),
            pl.BlockSpec((tm, D_MODEL), lambda i: (i, 0)),
            pl.BlockSpec((1, 6, D_MODEL), mod_idx),
            pl.BlockSpec((D_MODEL, D_MODEL), lambda i: (0, 0), pipeline_mode=pl.Buffered(1)),
            pl.BlockSpec((1, D_MODEL), lambda i: (0, 0)),
            pl.BlockSpec((1, D_MODEL), lambda i: (0, 0)),
        ],
        out_specs=[
            pl.BlockSpec((tm, D_MODEL), lambda i: (i, 0)),
            pl.BlockSpec((tm, D_MODEL), lambda i: (i, 0)),
        ],
        out_shape=[
            jax.ShapeDtypeStruct((t, D_MODEL), F32),
            jax.ShapeDtypeStruct((t, D_MODEL), BF16),
        ],
        compiler_params=_cparams(("arbitrary",)),
        name="proj",
    )(o_a, o_b, x2d, mods, w_out_bf, g_post, g_ffn)


def _ffn_kernel(h2_ref, x1_ref, mod_ref, wv_ref, wg_ref, cwv_ref, cwg_ref, cbv_ref, cbg_ref,
                wd_ref, gpost_ref, out_ref, *, tm, seq):
    j = pl.program_id(1)
    h2 = h2_ref[...]
    pos = lax.broadcasted_iota(jnp.int32, (tm, 1), 0) & (seq - 1)
    first = pos == 0
    last = pos == seq - 1

    def conv(u, cw_ref, cb_ref):
        up = jnp.where(first, 0.0, pltpu.roll(u, 1, 0))
        un = jnp.where(last, 0.0, pltpu.roll(u, tm - 1, 0))
        return up * cw_ref[0:1, :] + u * cw_ref[1:2, :] + un * cw_ref[2:3, :] + cb_ref[...]

    val = conv(jnp.dot(h2, wv_ref[...], preferred_element_type=F32), cwv_ref, cbv_ref)
    gate = conv(jnp.dot(h2, wg_ref[...], preferred_element_type=F32), cwg_ref, cbg_ref)
    act = (gate / (1.0 + jnp.exp(-gate)) * val).astype(BF16)
    contrib = jnp.dot(act, wd_ref[...], preferred_element_type=F32)

    @pl.when(j == 0)
    def _():
        out_ref[...] = contrib

    @pl.when(j > 0)
    def _():
        out_ref[...] += contrib

    @pl.when(j == N_FFN_TILES - 1)
    def _():
        ga2 = mod_ref[0, 5:6, :]
        out_ref[...] = x1_ref[...] + ga2 * (_rms(out_ref[...]) * gpost_ref[...])


def _ffn(h2, x1, mods, w_up_bf, conv_w, conv_b, w_down_bf, g_post, *, seq, tm):
    t = x1.shape[0]
    nmod = mods.shape[0]
    mod_idx = (lambda i, j: (0, 0, 0)) if nmod == 1 else (lambda i, j: (i * tm // seq, 0, 0))
    nj = N_FFN_TILES
    return pl.pallas_call(
        functools.partial(_ffn_kernel, tm=tm, seq=seq),
        grid=(t // tm, nj),
        in_specs=[
            pl.BlockSpec((tm, D_MODEL), lambda i, j: (i, 0), pipeline_mode=pl.Buffered(1)),
            pl.BlockSpec((tm, D_MODEL), lambda i, j: (i, 0), pipeline_mode=pl.Buffered(1)),
            pl.BlockSpec((1, 6, D_MODEL), mod_idx),
            pl.BlockSpec((D_MODEL, FFN_TF), lambda i, j: (0, j)),
            pl.BlockSpec((D_MODEL, FFN_TF), lambda i, j: (0, nj + j)),
            pl.BlockSpec((3, FFN_TF), lambda i, j: (0, j)),
            pl.BlockSpec((3, FFN_TF), lambda i, j: (0, nj + j)),
            pl.BlockSpec((1, FFN_TF), lambda i, j: (0, j)),
            pl.BlockSpec((1, FFN_TF), lambda i, j: (0, nj + j)),
            pl.BlockSpec((FFN_TF, D_MODEL), lambda i, j: (j, 0)),
            pl.BlockSpec((1, D_MODEL), lambda i, j: (0, 0)),
        ],
        out_specs=pl.BlockSpec((tm, D_MODEL), lambda i, j: (i, 0)),
        out_shape=jax.ShapeDtypeStruct((t, D_MODEL), F32),
        compiler_params=_cparams(("arbitrary", "arbitrary")),
        name="ffn",
    )(h2, x1, mods, w_up_bf, w_up_bf, conv_w, conv_w, conv_b, conv_b, w_down_bf, g_post)


def _rope_tables(n_tok):
    half = HEAD_DIM // 2
    t = np.arange(n_tok)
    freqs = ROPE_THETA ** (-np.arange(0, half, 2, dtype=np.float64) / half)
    lane = np.arange(HEAD_DIM)
    pos = np.where(lane[None, :] < half, (t // GRID_W)[:, None], (t % GRID_W)[:, None])
    ang = pos * freqs[lane % (half // 2)][None, :]
    sign = np.where((lane % half) < half // 2, -1.0, 1.0)[None, :]
    return (jnp.asarray(np.cos(ang), F32), jnp.asarray(np.sin(ang) * sign, F32))


def kernel(x_prompt, x_sample, c, cache_a_k, cache_a_v, cache_b_k, cache_b_v, c_ctx, w_mod, b_mod,
           g_attn_pre, g_attn_post, g_ffn_pre, g_ffn_post, w_in, rpb, g_qnorm, g_knorm, w_out,
           w_up, conv_w, conv_b, w_down):
    batch, seq, _ = x_prompt.shape
    dec_batch, dec_seq, _ = x_sample.shape
    depth = w_mod.shape[0]
    assert depth == 1 and dec_seq == GRID_H * GRID_W

    xp = x_prompt.reshape(batch * seq, D_MODEL)
    xs = x_sample.reshape(dec_batch * dec_seq, D_MODEL)
    l = 0
    conds = jnp.concatenate(
        [c_ctx[None, :], c, jnp.zeros((16 - 1 - dec_batch, D_MODEL), F32)], axis=0)
    mods = _modulation(conds, w_mod[l], b_mod[l]).reshape(16, 6, D_MODEL)
    mod_ctx = mods[0:1]
    mod_lat = mods[1:1 + dec_batch]

    w_in_bf = w_in[l].astype(BF16)
    w_out_bf = w_out[l].astype(BF16)
    w_up_bf = w_up[l].astype(BF16)
    w_down_bf = w_down[l].astype(BF16)
    g_pre = g_attn_pre[l].reshape(1, D_MODEL)
    g_post = g_attn_post[l].reshape(1, D_MODEL)
    g_fpre = g_ffn_pre[l].reshape(1, D_MODEL)
    g_fpost = g_ffn_post[l].reshape(1, D_MODEL)
    g_q = g_qnorm[l].reshape(1, HEAD_DIM)
    g_k = g_knorm[l].reshape(1, HEAD_DIM)
    cb = conv_b[l].reshape(1, 2 * D_FF)

    qkvh_c, st_ak, st_av, st_bk, st_bv = _qkv(
        xp, mod_ctx, g_pre, w_in_bf, g_q, g_k, is_ctx=True, seq=seq)
    o_c = _attn_ctx(qkvh_c, seq=seq)
    x1_c, h2_c = _proj(o_c, o_c, xp, mod_ctx, w_out_bf, g_post, g_fpre, tiles_per_mod=1)
    y_c = _ffn(h2_c, x1_c, mod_ctx, w_up_bf, conv_w[l], cb, w_down_bf, g_fpost, seq=seq, tm=1024)

    qkvh_s = _qkv(xs, mod_lat, g_pre, w_in_bf, g_q, g_k, is_ctx=False, seq=dec_seq,
                  rope_tabs=_rope_tables(dec_seq))[0]
    bias = _bias_table(rpb[l])
    o_sa = _attn_lat_a(qkvh_s, cache_a_k, cache_a_v, bias)
    o_sb = _attn_lat_b(qkvh_s, cache_b_k, cache_b_v)
    x1_s, h2_s = _proj(o_sa, o_sb, xs, mod_lat, w_out_bf, g_post, g_fpre,
                       tiles_per_mod=dec_seq // 512)
    y_s = _ffn(h2_s, x1_s, mod_lat, w_up_bf, conv_w[l], cb, w_down_bf, g_fpost,
               seq=dec_seq, tm=1024)

    return (y_c.reshape(batch, seq, D_MODEL), y_s.reshape(dec_batch, dec_seq, D_MODEL),
            st_ak, st_av, st_bk, st_bv)
```

```python
import functools

import numpy as np
import jax
import jax.numpy as jnp
from jax import lax
from jax.experimental import pallas as pl
from jax.experimental.pallas import tpu as pltpu

F32 = jnp.float32
BF16 = jnp.bfloat16

D_MODEL = 2048
HEAD_DIM = 128
N_HEADS_A = 8
N_HEADS_B = 8
N_KV_B = 2
GRID_W = 64
GRID_H = 16
WIN_H = 8
WIN_W = 16
D_FF = 5632
ROPE_THETA = 10000.0
EPS = 1e-6
NEG_INF = -1e30
SCALE = HEAD_DIM ** -0.5
IN_COLS = 4608
N_SLOTS = IN_COLS // HEAD_DIM
QKV_TN = 512
N_QKV_TILES = IN_COLS // QKV_TN
FFN_TF = 512
FFN_CW = 256
N_FFN_TILES = D_FF // FFN_TF
PROJ_ROWS = 128
HALF_Q = 512
BAND_K = 768
VMEM_LIMIT = 56 * 1024 * 1024


def _cparams(sem):
    return pltpu.CompilerParams(dimension_semantics=sem, vmem_limit_bytes=VMEM_LIMIT)


def _rms(x):
    return x * lax.rsqrt(jnp.mean(x * x, axis=-1, keepdims=True) + EPS)


def _mod_kernel(c_ref, w_ref, b_ref, o_ref):
    c = c_ref[...]
    a = (c / (1.0 + jnp.exp(-c))).astype(BF16)
    o_ref[...] = jnp.dot(a, w_ref[...].astype(BF16), preferred_element_type=F32) + b_ref[...]


def _modulation(conds, w_mod, b_mod):
    tn = 1024
    n = w_mod.shape[1]
    return pl.pallas_call(
        _mod_kernel,
        grid=(n // tn,),
        in_specs=[
            pl.BlockSpec((16, D_MODEL), lambda j: (0, 0)),
            pl.BlockSpec((D_MODEL, tn), lambda j: (0, j)),
            pl.BlockSpec((1, tn), lambda j: (0, j)),
        ],
        out_specs=pl.BlockSpec((16, tn), lambda j: (0, j)),
        out_shape=jax.ShapeDtypeStruct((16, n), F32),
        compiler_params=_cparams(("arbitrary",)),
        name="modulation",
    )(conds, w_mod, b_mod.reshape(1, n))


def _bias_kernel(rpb_ref, o_ref):
    h = pl.program_id(0)
    qc = lax.broadcasted_iota(jnp.int32, (GRID_W, 128), 0)
    lane = lax.broadcasted_iota(jnp.int32, (GRID_W, 128), 1)
    kc = lane & (GRID_W - 1)
    dmat = kc - qc + (WIN_W - 1)
    cs = jnp.clip(qc - WIN_W // 2, 0, GRID_W - WIN_W)
    col_valid = (kc >= cs) & (kc < cs + WIN_W)
    neg = jnp.full((GRID_W, 128), NEG_INF, F32)

    g = []
    for dr in range(2 * WIN_H - 1):
        acc = neg
        for d in range(2 * WIN_W - 1):
            acc = jnp.where(dmat == d, rpb_ref[h, dr * (2 * WIN_W - 1) + d], acc)
        g.append(jnp.where(col_valid, acc, neg))

    def row_tile(r, kr):
        rs = min(max(r - WIN_H // 2, 0), GRID_H - WIN_H)
        if rs <= kr < rs + WIN_H:
            return g[kr - r + WIN_H - 1]
        return neg

    for half in range(2):
        for lr in range(HALF_Q // GRID_W):
            r = 8 * half + lr
            for p in range(BAND_K // 128):
                kr0 = 4 * half + 2 * p
                tile = jnp.where(lane < GRID_W, row_tile(r, kr0), row_tile(r, kr0 + 1))
                o_ref[0, half, lr * GRID_W:(lr + 1) * GRID_W, p * 128:(p + 1) * 128] = tile


def _bias_table(rpb_l):
    rpb2 = rpb_l.reshape(N_HEADS_A, (2 * WIN_H - 1) * (2 * WIN_W - 1))
    return pl.pallas_call(
        _bias_kernel,
        grid=(N_HEADS_A,),
        in_specs=[pl.BlockSpec(memory_space=pltpu.SMEM)],
        out_specs=pl.BlockSpec((1, 2, HALF_Q, BAND_K), lambda h: (h, 0, 0, 0)),
        out_shape=jax.ShapeDtypeStruct((N_HEADS_A, 2, HALF_Q, BAND_K), F32),
        compiler_params=_cparams(("arbitrary",)),
        name="bias_table",
    )(rpb2)


def _rope(y, cos, sin_signed):
    lane = lax.broadcasted_iota(jnp.int32, y.shape, 1)
    swapped = jnp.where((lane & 63) < 32, pltpu.roll(y, 96, 1), pltpu.roll(y, 32, 1))
    return y * cos + swapped * sin_signed


def _qkv_kernel(*refs, is_ctx, tm, seq):
    if is_ctx:
        (x_ref, mod_ref, gpre_ref, w_ref, gq_ref, gk_ref,
         out_ref, sak_ref, sav_ref, sbk_ref, sbv_ref) = refs
    else:
        (x_ref, mod_ref, gpre_ref, w_ref, gq_ref, gk_ref, cos_ref, sin_ref, out_ref) = refs
    nb = max(tm // seq, 1)
    rows_per_state = min(seq, tm)
    heads_per_tile = QKV_TN // HEAD_DIM
    sh1 = mod_ref[0, 0:1, :]
    sc1 = mod_ref[0, 1:2, :]
    h = ((_rms(x_ref[...]) * gpre_ref[...]) * (1.0 + sc1) + sh1).astype(BF16)
    gq = gq_ref[...]
    gk = gk_ref[...]

    def project(n):
        return jnp.dot(h, w_ref[:, n * QKV_TN:(n + 1) * QKV_TN], preferred_element_type=F32)

    def store_state(s_ref, slot, val):
        for bb in range(nb):
            s_ref[bb, 0, slot] = val[bb * rows_per_state:(bb + 1) * rows_per_state]

    def rope(y):
        if is_ctx:
            return y
        return _rope(y, cos_ref[...], sin_ref[...])

    def epilogue(n, res):
        for hh in range(heads_per_tile):
            slot = n * heads_per_tile + hh
            v = res[:, hh * HEAD_DIM:(hh + 1) * HEAD_DIM]
            if slot < 8:
                out_ref[slot] = (v * SCALE).astype(BF16)
            elif slot < 24:
                out_ref[slot] = v.astype(BF16)
                if is_ctx:
                    store_state(sak_ref if slot < 16 else sav_ref, slot % 8, v)
            elif slot < 32:
                out_ref[slot] = (rope(_rms(v) * gq) * SCALE).astype(BF16)
            elif slot < 34:
                y = _rms(v) * gk
                if is_ctx:
                    store_state(sbk_ref, slot - 32, y)
                out_ref[slot] = rope(y).astype(BF16)
            else:
                out_ref[slot] = v.astype(BF16)
                if is_ctx:
                    store_state(sbv_ref, slot - 34, v)

    res = project(0)
    for n in range(N_QKV_TILES):
        nxt = project(n + 1) if n + 1 < N_QKV_TILES else None
        epilogue(n, res)
        res = nxt


def _qkv(x2d, mods, g_pre, w_in_bf, g_q, g_k, *, is_ctx, seq, rope_tabs=None):
    t = x2d.shape[0]
    tm = 256 if is_ctx else 512
    nb = max(tm // seq, 1)
    tiles_per_seq = max(seq // tm, 1)
    nmod = mods.shape[0]
    mod_idx = (lambda i: (0, 0, 0)) if nmod == 1 else (lambda i: (i // tiles_per_seq, 0, 0))
    in_specs = [
        pl.BlockSpec((tm, D_MODEL), lambda i: (i, 0)),
        pl.BlockSpec((1, 6, D_MODEL), mod_idx),
        pl.BlockSpec((1, D_MODEL), lambda i: (0, 0)),
        pl.BlockSpec((D_MODEL, IN_COLS), lambda i: (0, 0), pipeline_mode=pl.Buffered(1)),
        pl.BlockSpec((1, HEAD_DIM), lambda i: (0, 0)),
        pl.BlockSpec((1, HEAD_DIM), lambda i: (0, 0)),
    ]
    args = [x2d, mods, g_pre, w_in_bf, g_q, g_k]
    out_specs = [pl.BlockSpec((N_SLOTS, tm, HEAD_DIM), lambda i: (0, i, 0))]
    out_shape = [jax.ShapeDtypeStruct((N_SLOTS, t, HEAD_DIM), BF16)]
    if is_ctx:
        assert tm % seq == 0
        nbatch = t // seq
        for nh in (N_HEADS_A, N_HEADS_A, N_KV_B, N_KV_B):
            out_specs.append(pl.BlockSpec((nb, 1, nh, seq, HEAD_DIM), lambda i: (i, 0, 0, 0, 0)))
            out_shape.append(jax.ShapeDtypeStruct((nbatch, 1, nh, seq, HEAD_DIM), F32))
    else:
        rope_idx = lambda i: (i % tiles_per_seq, 0)
        in_specs += [pl.BlockSpec((tm, HEAD_DIM), rope_idx), pl.BlockSpec((tm, HEAD_DIM), rope_idx)]
        args += list(rope_tabs)
    return pl.pallas_call(
        functools.partial(_qkv_kernel, is_ctx=is_ctx, tm=tm, seq=seq),
        grid=(t // tm,),
        in_specs=in_specs,
        out_specs=out_specs,
        out_shape=out_shape,
        compiler_params=_cparams(("arbitrary",)),
        name="qkv_ctx" if is_ctx else "qkv_lat",
    )(*args)


def _qk(q, k):
    return lax.dot_general(q, k, (((1,), (1,)), ((), ())), preferred_element_type=F32)


def _softmax_pv(scores, values):
    m = functools.reduce(jnp.maximum, [jnp.max(s, axis=-1, keepdims=True) for s in scores])
    acc = None
    den = None
    for s, v in zip(scores, values):
        e = jnp.exp(s - m)
        l = jnp.sum(e, axis=-1, keepdims=True)
        o = jnp.dot(e.astype(BF16), v, preferred_element_type=F32)
        acc = o if acc is None else acc + o
        den = l if den is None else den + l
    return acc / den


def _attn_ctx_kernel(qkv_ref, o_ref, *, nb, seq):
    for bb in range(nb):
        rows = slice(bb * seq, (bb + 1) * seq)
        for h in range(N_HEADS_A):
            q = qkv_ref[h, rows, :]
            k = qkv_ref[N_HEADS_A + h, rows, :]
            v = qkv_ref[2 * N_HEADS_A + h, rows, :]
            o = _softmax_pv([_qk(q, k)], [v])
            o_ref[rows, h * HEAD_DIM:(h + 1) * HEAD_DIM] = o.astype(BF16)
        group = N_HEADS_B // N_KV_B
        for j in range(N_KV_B):
            k = qkv_ref[32 + j, rows, :]
            v = qkv_ref[34 + j, rows, :]
            for g in range(group):
                hq = j * group + g
                q = qkv_ref[24 + hq, rows, :]
                o = _softmax_pv([_qk(q, k)], [v])
                c0 = (N_HEADS_A + hq) * HEAD_DIM
                o_ref[rows, c0:c0 + HEAD_DIM] = o.astype(BF16)


def _attn_ctx(qkvh, *, seq):
    t = qkvh.shape[1]
    nb = 2
    rows = nb * seq
    return pl.pallas_call(
        functools.partial(_attn_ctx_kernel, nb=nb, seq=seq),
        grid=(t // rows,),
        in_specs=[pl.BlockSpec((N_SLOTS, rows, HEAD_DIM), lambda i: (0, i, 0))],
        out_specs=pl.BlockSpec((rows, D_MODEL), lambda i: (i, 0)),
        out_shape=jax.ShapeDtypeStruct((t, D_MODEL), BF16),
        compiler_params=_cparams(("arbitrary",)),
        name="attn_ctx",
    )(qkvh)


def _attn_lat_a_kernel(q_ref, k0_ref, k1_ref, k2_ref, v0_ref, v1_ref, v2_ref,
                       ck_ref, cv_ref, bias_ref, o_ref):
    for h in range(N_HEADS_A):
        q = q_ref[h]
        s_nb = jnp.concatenate([_qk(q, k0_ref[h]), _qk(q, k1_ref[h]), _qk(q, k2_ref[h])], axis=1)
        s_nb = s_nb + bias_ref[h, 0]
        s_cx = _qk(q, ck_ref[0, 0, h].astype(BF16))
        v_nb = jnp.concatenate([v0_ref[h], v1_ref[h], v2_ref[h]], axis=0)
        o = _softmax_pv([s_nb, s_cx], [v_nb, cv_ref[0, 0, h].astype(BF16)])
        o_ref[:, h * HEAD_DIM:(h + 1) * HEAD_DIM] = o.astype(BF16)


def _attn_lat_a(qkvh, cache_k, cache_v, bias):
    t = qkvh.shape[1]
    nbatch = t // 1024
    kq = 256

    def kspec(slot_blk, j):
        return pl.BlockSpec((8, kq, HEAD_DIM), lambda hf, b: (slot_blk, b * 4 + hf + j, 0))

    cache_spec = pl.BlockSpec((1, 1, N_HEADS_A, 256, HEAD_DIM), lambda hf, b: (b, 0, 0, 0, 0))
    return pl.pallas_call(
        _attn_lat_a_kernel,
        grid=(2, nbatch),
        in_specs=[
            pl.BlockSpec((8, HALF_Q, HEAD_DIM), lambda hf, b: (0, b * 2 + hf, 0)),
            kspec(1, 0), kspec(1, 1), kspec(1, 2),
            kspec(2, 0), kspec(2, 1), kspec(2, 2),
            cache_spec, cache_spec,
            pl.BlockSpec((N_HEADS_A, 1, HALF_Q, BAND_K), lambda hf, b: (0, hf, 0, 0)),
        ],
        out_specs=pl.BlockSpec((HALF_Q, N_HEADS_A * HEAD_DIM), lambda hf, b: (b * 2 + hf, 0)),
        out_shape=jax.ShapeDtypeStruct((t, N_HEADS_A * HEAD_DIM), BF16),
        compiler_params=_cparams(("arbitrary", "arbitrary")),
        name="attn_lat_a",
    )(qkvh, qkvh, qkvh, qkvh, qkvh, qkvh, qkvh, cache_k, cache_v, bias)


def _attn_lat_b_kernel(q_ref, k_ref, v_ref, ck_ref, cv_ref, o_ref):
    k = k_ref[0]
    v = v_ref[0]
    ck = ck_ref[0, 0, 0].astype(BF16)
    cv = cv_ref[0, 0, 0].astype(BF16)
    for g in range(N_HEADS_B // N_KV_B):
        for qh in range(2):
            rows = slice(qh * 512, (qh + 1) * 512)
            q = q_ref[g, rows, :]
            o = _softmax_pv([_qk(q, k), _qk(q, ck)], [v, cv])
            o_ref[rows, g * HEAD_DIM:(g + 1) * HEAD_DIM] = o.astype(BF16)


def _attn_lat_b(qkvh, cache_k, cache_v):
    t = qkvh.shape[1]
    nbatch = t // 1024
    cache_spec = pl.BlockSpec((1, 1, 1, 256, HEAD_DIM), lambda b, j: (b, 0, j, 0, 0))
    return pl.pallas_call(
        _attn_lat_b_kernel,
        grid=(nbatch, N_KV_B),
        in_specs=[
            pl.BlockSpec((4, 1024, HEAD_DIM), lambda b, j: (6 + j, b, 0)),
            pl.BlockSpec((1, 1024, HEAD_DIM), lambda b, j: (32 + j, b, 0)),
            pl.BlockSpec((1, 1024, HEAD_DIM), lambda b, j: (34 + j, b, 0)),
            cache_spec, cache_spec,
        ],
        out_specs=pl.BlockSpec((1024, 512), lambda b, j: (b, j)),
        out_shape=jax.ShapeDtypeStruct((t, N_HEADS_B * HEAD_DIM), BF16),
        compiler_params=_cparams(("arbitrary", "arbitrary")),
        name="attn_lat_b",
    )(qkvh, qkvh, qkvh, cache_k, cache_v)


def _proj_kernel(oa_ref, ob_ref, x_ref, mod_ref, w_ref, gpost_ref, gffn_ref, x1_ref, h2_ref):
    ga1 = mod_ref[0, 2:3, :]
    sh2 = mod_ref[0, 3:4, :]
    sc2 = mod_ref[0, 4:5, :]
    tm, ka = oa_ref.shape
    chunks = [slice(r, r + PROJ_ROWS) for r in range(0, tm, PROJ_ROWS)]

    def project(rows):
        return (jnp.dot(oa_ref[rows, :], w_ref[:ka, :], preferred_element_type=F32)
                + jnp.dot(ob_ref[rows, :], w_ref[ka:, :], preferred_element_type=F32))

    t = project(chunks[0])
    for r, rows in enumerate(chunks):
        nxt = project(chunks[r + 1]) if r + 1 < len(chunks) else None
        x1 = x_ref[rows, :] + ga1 * (_rms(t) * gpost_ref[...])
        x1_ref[rows, :] = x1
        h2_ref[rows, :] = (_rms(x1) * gffn_ref[...] * (1.0 + sc2) + sh2).astype(BF16)
        t = nxt


def _proj(o_a, o_b, x2d, mods, w_out_bf, g_post, g_ffn, *, tiles_per_mod):
    t = x2d.shape[0]
    tm = 512
    nmod = mods.shape[0]
    mod_idx = (lambda i: (0, 0, 0)) if nmod == 1 else (lambda i: (i // tiles_per_mod, 0, 0))
    ka = D_MODEL // 2
    ob_idx = (lambda i: (i, 1)) if o_b.shape[1] == D_MODEL else (lambda i: (i, 0))
    return pl.pallas_call(
        _proj_kernel,
        grid=(t // tm,),
        in_specs=[
            pl.BlockSpec((tm, ka), lambda i: (i, 0)),
            pl.BlockSpec((tm, ka), ob_idx),
            pl.BlockSpec((tm, D_MODEL), lambda i: (i, 0)),
            pl.BlockSpec((1, 6, D_MODEL), mod_idx),
            pl.BlockSpec((D_MODEL, D_MODEL), lambda i: (0, 0), pipeline_mode=pl.Buffered(1)),
            pl.BlockSpec((1, D_MODEL), lambda i: (0, 0)),
            pl.BlockSpec((1, D_MODEL), lambda i: (0, 0)),
        ],
        out_specs=[
            pl.BlockSpec((tm, D_MODEL), lambda i: (i, 0)),
            pl.BlockSpec((tm, D_MODEL), lambda i: (i, 0)),
        ],
        out_shape=[
            jax.ShapeDtypeStruct((t, D_MODEL), F32),
            jax.ShapeDtypeStruct((t, D_MODEL), BF16),
        ],
        compiler_params=_cparams(("arbitrary",)),
        name="proj",
    )(o_a, o_b, x2d, mods, w_out_bf, g_post, g_ffn)


def _ffn_kernel(h2_ref, x1_ref, mod_ref, wv_ref, wg_ref, cwv_ref, cwg_ref, cbv_ref, cbg_ref,
                wd_ref, gpost_ref, out_ref, *u_scr, tm, seq):
    j = pl.program_id(1)
    nseq = tm // seq
    stride = seq + 8

    @pl.when(j == 0)
    def _():
        out_ref[...] = jnp.zeros_like(out_ref)
        for scr in u_scr:
            for s in range(nseq + 1):
                scr[s * stride:s * stride + 8, :] = jnp.zeros((8, FFN_CW), F32)

    h2 = h2_ref[...]
    chunks = [slice(c * FFN_CW, (c + 1) * FFN_CW) for c in range(FFN_TF // FFN_CW)]
    for c, cols in enumerate(chunks):
        for k, w_ref in enumerate((wv_ref, wg_ref)):
            u = jnp.dot(h2, w_ref[:, cols], preferred_element_type=F32)
            for s in range(nseq):
                u_scr[2 * c + k][8 + s * stride:8 + s * stride + seq, :] = u[s * seq:(s + 1) * seq]

    def conv(scr, base, cw_ref, cb_ref):
        up = scr[base - 1:base - 1 + seq, :]
        mid = scr[base:base + seq, :]
        un = scr[base + 1:base + 1 + seq, :]
        return up * cw_ref[0:1, :] + (mid * cw_ref[1:2, :] + cb_ref[...]) + un * cw_ref[2:3, :]

    for c, cols in enumerate(chunks):
        acts = []
        for s in range(nseq):
            base = 8 + s * stride
            val = conv(u_scr[2 * c], base, cwv_ref.at[:, cols], cbv_ref.at[:, cols])
            gate = conv(u_scr[2 * c + 1], base, cwg_ref.at[:, cols], cbg_ref.at[:, cols])
            acts.append((gate / (1.0 + jnp.exp(-gate)) * val).astype(BF16))
        act = acts[0] if nseq == 1 else jnp.concatenate(acts, axis=0)
        out_ref[...] += jnp.dot(act, wd_ref[cols, :], preferred_element_type=F32)

    @pl.when(j == N_FFN_TILES - 1)
    def _():
        ga2 = mod_ref[0, 5:6, :]
        out_ref[...] = x1_ref[...] + ga2 * (_rms(out_ref[...]) * gpost_ref[...])


def _ffn(h2, x1, mods, w_up_bf, conv_w, conv_b, w_down_bf, g_post, *, seq, tm):
    t = x1.shape[0]
    nmod = mods.shape[0]
    mod_idx = (lambda i, j: (0, 0, 0)) if nmod == 1 else (lambda i, j: (i * tm // seq, 0, 0))
    nj = N_FFN_TILES
    return pl.pallas_call(
        functools.partial(_ffn_kernel, tm=tm, seq=seq),
        grid=(t // tm, nj),
        in_specs=[
            pl.BlockSpec((tm, D_MODEL), lambda i, j: (i, 0), pipeline_mode=pl.Buffered(1)),
            pl.BlockSpec((tm, D_MODEL), lambda i, j: (i, 0), pipeline_mode=pl.Buffered(1)),
            pl.BlockSpec((1, 6, D_MODEL), mod_idx),
            pl.BlockSpec((D_MODEL, FFN_TF), lambda i, j: (0, j)),
            pl.BlockSpec((D_MODEL, FFN_TF), lambda i, j: (0, nj + j)),
            pl.BlockSpec((3, FFN_TF), lambda i, j: (0, j)),
            pl.BlockSpec((3, FFN_TF), lambda i, j: (0, nj + j)),
            pl.BlockSpec((1, FFN_TF), lambda i, j: (0, j)),
            pl.BlockSpec((1, FFN_TF), lambda i, j: (0, nj + j)),
            pl.BlockSpec((FFN_TF, D_MODEL), lambda i, j: (j, 0)),
            pl.BlockSpec((1, D_MODEL), lambda i, j: (0, 0)),
        ],
        out_specs=pl.BlockSpec((tm, D_MODEL), lambda i, j: (i, 0)),
        out_shape=jax.ShapeDtypeStruct((t, D_MODEL), F32),
        scratch_shapes=[pltpu.VMEM((8 + (tm // seq) * (seq + 8), FFN_CW), F32)
                        for _ in range(2 * (FFN_TF // FFN_CW))],
        compiler_params=_cparams(("arbitrary", "arbitrary")),
        name="ffn",
    )(h2, x1, mods, w_up_bf, w_up_bf, conv_w, conv_w, conv_b, conv_b, w_down_bf, g_post)


def _rope_tables(n_tok):
    half = HEAD_DIM // 2
    t = np.arange(n_tok)
    freqs = ROPE_THETA ** (-np.arange(0, half, 2, dtype=np.float64) / half)
    lane = np.arange(HEAD_DIM)
    pos = np.where(lane[None, :] < half, (t // GRID_W)[:, None], (t % GRID_W)[:, None])
    ang = pos * freqs[lane % (half // 2)][None, :]
    sign = np.where((lane % half) < half // 2, -1.0, 1.0)[None, :]
    return (jnp.asarray(np.cos(ang), F32), jnp.asarray(np.sin(ang) * sign, F32))


def kernel(x_prompt, x_sample, c, cache_a_k, cache_a_v, cache_b_k, cache_b_v, c_ctx, w_mod, b_mod,
           g_attn_pre, g_attn_post, g_ffn_pre, g_ffn_post, w_in, rpb, g_qnorm, g_knorm, w_out,
           w_up, conv_w, conv_b, w_down):
    batch, seq, _ = x_prompt.shape
    dec_batch, dec_seq, _ = x_sample.shape
    depth = w_mod.shape[0]
    assert depth == 1 and dec_seq == GRID_H * GRID_W

    xp = x_prompt.reshape(batch * seq, D_MODEL)
    xs = x_sample.reshape(dec_batch * dec_seq, D_MODEL)
    l = 0
    conds = jnp.concatenate(
        [c_ctx[None, :], c, jnp.zeros((16 - 1 - dec_batch, D_MODEL), F32)], axis=0)
    mods = _modulation(conds, w_mod[l], b_mod[l]).reshape(16, 6, D_MODEL)
    mod_ctx = mods[0:1]
    mod_lat = mods[1:1 + dec_batch]

    w_in_bf = w_in[l].astype(BF16)
    w_out_bf = w_out[l].astype(BF16)
    w_up_bf = w_up[l].astype(BF16)
    w_down_bf = w_down[l].astype(BF16)
    g_pre = g_attn_pre[l].reshape(1, D_MODEL)
    g_post = g_attn_post[l].reshape(1, D_MODEL)
    g_fpre = g_ffn_pre[l].reshape(1, D_MODEL)
    g_fpost = g_ffn_post[l].reshape(1, D_MODEL)
    g_q = g_qnorm[l].reshape(1, HEAD_DIM)
    g_k = g_knorm[l].reshape(1, HEAD_DIM)
    cb = conv_b[l].reshape(1, 2 * D_FF)

    qkvh_c, st_ak, st_av, st_bk, st_bv = _qkv(
        xp, mod_ctx, g_pre, w_in_bf, g_q, g_k, is_ctx=True, seq=seq)
    o_c = _attn_ctx(qkvh_c, seq=seq)
    x1_c, h2_c = _proj(o_c, o_c, xp, mod_ctx, w_out_bf, g_post, g_fpre, tiles_per_mod=1)
    y_c = _ffn(h2_c, x1_c, mod_ctx, w_up_bf, conv_w[l], cb, w_down_bf, g_fpost, seq=seq, tm=1024)

    qkvh_s = _qkv(xs, mod_lat, g_pre, w_in_bf, g_q, g_k, is_ctx=False, seq=dec_seq,
                  rope_tabs=_rope_tables(dec_seq))[0]
    bias = _bias_table(rpb[l])
    o_sa = _attn_lat_a(qkvh_s, cache_a_k, cache_a_v, bias)
    o_sb = _attn_lat_b(qkvh_s, cache_b_k, cache_b_v)
    x1_s, h2_s = _proj(o_sa, o_sb, xs, mod_lat, w_out_bf, g_post, g_fpre,
                       tiles_per_mod=dec_seq // 512)
    y_s = _ffn(h2_s, x1_s, mod_lat, w_up_bf, conv_w[l], cb, w_down_bf, g_fpost,
               seq=dec_seq, tm=1024)

    return (y_c.reshape(batch, seq, D_MODEL), y_s.reshape(dec_batch, dec_seq, D_MODEL),
            st_ak, st_av, st_bk, st_bv)
```

```python
import functools

import numpy as np
import jax
import jax.numpy as jnp
from jax import lax
from jax.experimental import pallas as pl
from jax.experimental.pallas import tpu as pltpu

F32 = jnp.float32
BF16 = jnp.bfloat16

D_MODEL = 2048
HEAD_DIM = 128
N_HEADS_A = 8
N_HEADS_B = 8
N_KV_B = 2
GRID_W = 64
GRID_H = 16
WIN_H = 8
WIN_W = 16
D_FF = 5632
ROPE_THETA = 10000.0
EPS = 1e-6
NEG_INF = -1e30
SCALE = HEAD_DIM ** -0.5
IN_COLS = 4608
N_SLOTS = IN_COLS // HEAD_DIM
QKV_TN = 512
N_QKV_TILES = IN_COLS // QKV_TN
FFN_TF = 512
FFN_CW = 256
N_FFN_TILES = D_FF // FFN_TF
PROJ_ROWS = 128
HALF_Q = 512
BAND_K = 768
VMEM_LIMIT = 60 * 1024 * 1024


def _cparams(sem):
    return pltpu.CompilerParams(dimension_semantics=sem, vmem_limit_bytes=VMEM_LIMIT)


def _rms(x):
    return x * lax.rsqrt(jnp.mean(x * x, axis=-1, keepdims=True) + EPS)


def _mod_kernel(c_ref, w_ref, b_ref, o_ref):
    c = c_ref[...]
    a = (c / (1.0 + jnp.exp(-c))).astype(BF16)
    o_ref[...] = jnp.dot(a, w_ref[...].astype(BF16), preferred_element_type=F32) + b_ref[...]


def _modulation(conds, w_mod, b_mod):
    tn = 1024
    n = w_mod.shape[1]
    return pl.pallas_call(
        _mod_kernel,
        grid=(n // tn,),
        in_specs=[
            pl.BlockSpec((16, D_MODEL), lambda j: (0, 0)),
            pl.BlockSpec((D_MODEL, tn), lambda j: (0, j)),
            pl.BlockSpec((1, tn), lambda j: (0, j)),
        ],
        out_specs=pl.BlockSpec((16, tn), lambda j: (0, j)),
        out_shape=jax.ShapeDtypeStruct((16, n), F32),
        compiler_params=_cparams(("arbitrary",)),
        name="modulation",
    )(conds, w_mod, b_mod.reshape(1, n))


def _bias_kernel(rpb_ref, o_ref):
    h = pl.program_id(0)
    qc = lax.broadcasted_iota(jnp.int32, (GRID_W, 128), 0)
    lane = lax.broadcasted_iota(jnp.int32, (GRID_W, 128), 1)
    kc = lane & (GRID_W - 1)
    dmat = kc - qc + (WIN_W - 1)
    cs = jnp.clip(qc - WIN_W // 2, 0, GRID_W - WIN_W)
    col_valid = (kc >= cs) & (kc < cs + WIN_W)
    neg = jnp.full((GRID_W, 128), NEG_INF, F32)

    g = []
    for dr in range(2 * WIN_H - 1):
        acc = neg
        for d in range(2 * WIN_W - 1):
            acc = jnp.where(dmat == d, rpb_ref[h, dr * (2 * WIN_W - 1) + d], acc)
        g.append(jnp.where(col_valid, acc, neg))

    def row_tile(r, kr):
        rs = min(max(r - WIN_H // 2, 0), GRID_H - WIN_H)
        if rs <= kr < rs + WIN_H:
            return g[kr - r + WIN_H - 1]
        return neg

    for half in range(2):
        for lr in range(HALF_Q // GRID_W):
            r = 8 * half + lr
            for p in range(BAND_K // 128):
                kr0 = 4 * half + 2 * p
                tile = jnp.where(lane < GRID_W, row_tile(r, kr0), row_tile(r, kr0 + 1))
                o_ref[0, half, lr * GRID_W:(lr + 1) * GRID_W, p * 128:(p + 1) * 128] = tile


def _bias_table(rpb_l):
    rpb2 = rpb_l.reshape(N_HEADS_A, (2 * WIN_H - 1) * (2 * WIN_W - 1))
    return pl.pallas_call(
        _bias_kernel,
        grid=(N_HEADS_A,),
        in_specs=[pl.BlockSpec(memory_space=pltpu.SMEM)],
        out_specs=pl.BlockSpec((1, 2, HALF_Q, BAND_K), lambda h: (h, 0, 0, 0)),
        out_shape=jax.ShapeDtypeStruct((N_HEADS_A, 2, HALF_Q, BAND_K), F32),
        compiler_params=_cparams(("arbitrary",)),
        name="bias_table",
    )(rpb2)


def _rope(y, cos, sin_signed):
    lane = lax.broadcasted_iota(jnp.int32, y.shape, 1)
    swapped = jnp.where((lane & 63) < 32, pltpu.roll(y, 96, 1), pltpu.roll(y, 32, 1))
    return y * cos + swapped * sin_signed


def _qkv_kernel(*refs, is_ctx, tm, seq):
    if is_ctx:
        (x_ref, mod_ref, gpre_ref, w_ref, gq_ref, gk_ref,
         out_ref, sak_ref, sav_ref, sbk_ref, sbv_ref) = refs
    else:
        (x_ref, mod_ref, gpre_ref, w_ref, gq_ref, gk_ref, cos_ref, sin_ref, out_ref) = refs
    nb = max(tm // seq, 1)
    rows_per_state = min(seq, tm)
    heads_per_tile = QKV_TN // HEAD_DIM
    sh1 = mod_ref[0, 0:1, :]
    sc1 = mod_ref[0, 1:2, :]
    h = ((_rms(x_ref[...]) * gpre_ref[...]) * (1.0 + sc1) + sh1).astype(BF16)
    gq = gq_ref[...]
    gk = gk_ref[...]

    def project(n):
        return jnp.dot(h, w_ref[:, n * QKV_TN:(n + 1) * QKV_TN], preferred_element_type=F32)

    def store_state(s_ref, slot, val):
        for bb in range(nb):
            s_ref[bb, 0, slot] = val[bb * rows_per_state:(bb + 1) * rows_per_state]

    def rope(y):
        if is_ctx:
            return y
        return _rope(y, cos_ref[...], sin_ref[...])

    def epilogue(n, res):
        for hh in range(heads_per_tile):
            slot = n * heads_per_tile + hh
            v = res[:, hh * HEAD_DIM:(hh + 1) * HEAD_DIM]
            if slot < 8:
                out_ref[slot] = (v * SCALE).astype(BF16)
            elif slot < 24:
                out_ref[slot] = v.astype(BF16)
                if is_ctx:
                    store_state(sak_ref if slot < 16 else sav_ref, slot % 8, v)
            elif slot < 32:
                out_ref[slot] = (rope(_rms(v) * gq) * SCALE).astype(BF16)
            elif slot < 34:
                y = _rms(v) * gk
                if is_ctx:
                    store_state(sbk_ref, slot - 32, y)
                out_ref[slot] = rope(y).astype(BF16)
            else:
                out_ref[slot] = v.astype(BF16)
                if is_ctx:
                    store_state(sbv_ref, slot - 34, v)

    order = [6, 7, 8, 0, 1, 2, 3, 4, 5]
    res = project(order[0])
    for pos, n in enumerate(order):
        nxt = project(order[pos + 1]) if pos + 1 < len(order) else None
        epilogue(n, res)
        res = nxt


def _qkv(x2d, mods, g_pre, w_in_bf, g_q, g_k, *, is_ctx, seq, rope_tabs=None):
    t = x2d.shape[0]
    tm = 256 if is_ctx else 512
    nb = max(tm // seq, 1)
    tiles_per_seq = max(seq // tm, 1)
    nmod = mods.shape[0]
    mod_idx = (lambda i: (0, 0, 0)) if nmod == 1 else (lambda i: (i // tiles_per_seq, 0, 0))
    in_specs = [
        pl.BlockSpec((tm, D_MODEL), lambda i: (i, 0)),
        pl.BlockSpec((1, 6, D_MODEL), mod_idx),
        pl.BlockSpec((1, D_MODEL), lambda i: (0, 0)),
        pl.BlockSpec((D_MODEL, IN_COLS), lambda i: (0, 0), pipeline_mode=pl.Buffered(1)),
        pl.BlockSpec((1, HEAD_DIM), lambda i: (0, 0)),
        pl.BlockSpec((1, HEAD_DIM), lambda i: (0, 0)),
    ]
    args = [x2d, mods, g_pre, w_in_bf, g_q, g_k]
    out_specs = [pl.BlockSpec((N_SLOTS, tm, HEAD_DIM), lambda i: (0, i, 0))]
    out_shape = [jax.ShapeDtypeStruct((N_SLOTS, t, HEAD_DIM), BF16)]
    if is_ctx:
        assert tm % seq == 0
        nbatch = t // seq
        for nh in (N_HEADS_A, N_HEADS_A, N_KV_B, N_KV_B):
            out_specs.append(pl.BlockSpec((nb, 1, nh, seq, HEAD_DIM), lambda i: (i, 0, 0, 0, 0)))
            out_shape.append(jax.ShapeDtypeStruct((nbatch, 1, nh, seq, HEAD_DIM), F32))
    else:
        rope_idx = lambda i: (i % tiles_per_seq, 0)
        in_specs += [pl.BlockSpec((tm, HEAD_DIM), rope_idx), pl.BlockSpec((tm, HEAD_DIM), rope_idx)]
        args += list(rope_tabs)
    return pl.pallas_call(
        functools.partial(_qkv_kernel, is_ctx=is_ctx, tm=tm, seq=seq),
        grid=(t // tm,),
        in_specs=in_specs,
        out_specs=out_specs,
        out_shape=out_shape,
        compiler_params=_cparams(("arbitrary",)),
        name="qkv_ctx" if is_ctx else "qkv_lat",
    )(*args)


def _qk(q, k):
    return lax.dot_general(q, k, (((1,), (1,)), ((), ())), preferred_element_type=F32)


def _softmax_pv(scores, values):
    m = functools.reduce(jnp.maximum, [jnp.max(s, axis=-1, keepdims=True) for s in scores])
    acc = None
    den = None
    for s, v in zip(scores, values):
        e = jnp.exp(s - m)
        l = jnp.sum(e, axis=-1, keepdims=True)
        o = jnp.dot(e.astype(BF16), v, preferred_element_type=F32)
        acc = o if acc is None else acc + o
        den = l if den is None else den + l
    return acc / den


def _attn_ctx_kernel(qkv_ref, o_ref, *, nb, seq):
    for bb in range(nb):
        rows = slice(bb * seq, (bb + 1) * seq)
        for h in range(N_HEADS_A):
            q = qkv_ref[h, rows, :]
            k = qkv_ref[N_HEADS_A + h, rows, :]
            v = qkv_ref[2 * N_HEADS_A + h, rows, :]
            o = _softmax_pv([_qk(q, k)], [v])
            o_ref[rows, h * HEAD_DIM:(h + 1) * HEAD_DIM] = o.astype(BF16)
        group = N_HEADS_B // N_KV_B
        for j in range(N_KV_B):
            k = qkv_ref[32 + j, rows, :]
            v = qkv_ref[34 + j, rows, :]
            for g in range(group):
                hq = j * group + g
                q = qkv_ref[24 + hq, rows, :]
                o = _softmax_pv([_qk(q, k)], [v])
                c0 = (N_HEADS_A + hq) * HEAD_DIM
                o_ref[rows, c0:c0 + HEAD_DIM] = o.astype(BF16)


def _attn_ctx(qkvh, *, seq):
    t = qkvh.shape[1]
    nb = 2
    rows = nb * seq
    return pl.pallas_call(
        functools.partial(_attn_ctx_kernel, nb=nb, seq=seq),
        grid=(t // rows,),
        in_specs=[pl.BlockSpec((N_SLOTS, rows, HEAD_DIM), lambda i: (0, i, 0))],
        out_specs=pl.BlockSpec((rows, D_MODEL), lambda i: (i, 0)),
        out_shape=jax.ShapeDtypeStruct((t, D_MODEL), BF16),
        compiler_params=_cparams(("arbitrary",)),
        name="attn_ctx",
    )(qkvh)


def _attn_lat_a_kernel(q_ref, k0_ref, k1_ref, k2_ref, v0_ref, v1_ref, v2_ref,
                       ck_ref, cv_ref, bias_ref, o_ref):
    for h in range(N_HEADS_A):
        q = q_ref[h]
        s_nb = jnp.concatenate([_qk(q, k0_ref[h]), _qk(q, k1_ref[h]), _qk(q, k2_ref[h])], axis=1)
        s_nb = s_nb + bias_ref[h, 0]
        s_cx = _qk(q, ck_ref[0, 0, h].astype(BF16))
        v_nb = jnp.concatenate([v0_ref[h], v1_ref[h], v2_ref[h]], axis=0)
        o = _softmax_pv([s_nb, s_cx], [v_nb, cv_ref[0, 0, h].astype(BF16)])
        o_ref[:, h * HEAD_DIM:(h + 1) * HEAD_DIM] = o.astype(BF16)


def _attn_lat_a(qkvh, cache_k, cache_v, bias):
    t = qkvh.shape[1]
    nbatch = t // 1024
    kq = 256

    def kspec(slot_blk, j):
        return pl.BlockSpec((8, kq, HEAD_DIM), lambda hf, b: (slot_blk, b * 4 + hf + j, 0))

    cache_spec = pl.BlockSpec((1, 1, N_HEADS_A, 256, HEAD_DIM), lambda hf, b: (b, 0, 0, 0, 0))
    return pl.pallas_call(
        _attn_lat_a_kernel,
        grid=(2, nbatch),
        in_specs=[
            pl.BlockSpec((8, HALF_Q, HEAD_DIM), lambda hf, b: (0, b * 2 + hf, 0)),
            kspec(1, 0), kspec(1, 1), kspec(1, 2),
            kspec(2, 0), kspec(2, 1), kspec(2, 2),
            cache_spec, cache_spec,
            pl.BlockSpec((N_HEADS_A, 1, HALF_Q, BAND_K), lambda hf, b: (0, hf, 0, 0)),
        ],
        out_specs=pl.BlockSpec((HALF_Q, N_HEADS_A * HEAD_DIM), lambda hf, b: (b * 2 + hf, 0)),
        out_shape=jax.ShapeDtypeStruct((t, N_HEADS_A * HEAD_DIM), BF16),
        compiler_params=_cparams(("arbitrary", "arbitrary")),
        name="attn_lat_a",
    )(qkvh, qkvh, qkvh, qkvh, qkvh, qkvh, qkvh, cache_k, cache_v, bias)


def _attn_lat_b_kernel(q_ref, k_ref, v_ref, ck_ref, cv_ref, o_ref):
    k = k_ref[0]
    v = v_ref[0]
    ck = ck_ref[0, 0, 0].astype(BF16)
    cv = cv_ref[0, 0, 0].astype(BF16)
    for g in range(N_HEADS_B // N_KV_B):
        for qh in range(2):
            rows = slice(qh * 512, (qh + 1) * 512)
            q = q_ref[g, rows, :]
            o = _softmax_pv([_qk(q, k), _qk(q, ck)], [v, cv])
            o_ref[rows, g * HEAD_DIM:(g + 1) * HEAD_DIM] = o.astype(BF16)


def _attn_lat_b(qkvh, cache_k, cache_v):
    t = qkvh.shape[1]
    nbatch = t // 1024
    cache_spec = pl.BlockSpec((1, 1, 1, 256, HEAD_DIM), lambda b, j: (b, 0, j, 0, 0))
    return pl.pallas_call(
        _attn_lat_b_kernel,
        grid=(nbatch, N_KV_B),
        in_specs=[
            pl.BlockSpec((4, 1024, HEAD_DIM), lambda b, j: (6 + j, b, 0)),
            pl.BlockSpec((1, 1024, HEAD_DIM), lambda b, j: (32 + j, b, 0)),
            pl.BlockSpec((1, 1024, HEAD_DIM), lambda b, j: (34 + j, b, 0)),
            cache_spec, cache_spec,
        ],
        out_specs=pl.BlockSpec((1024, 512), lambda b, j: (b, j)),
        out_shape=jax.ShapeDtypeStruct((t, N_HEADS_B * HEAD_DIM), BF16),
        compiler_params=_cparams(("arbitrary", "arbitrary")),
        name="attn_lat_b",
    )(qkvh, qkvh, qkvh, cache_k, cache_v)


def _proj_kernel(oa_ref, ob_ref, x_ref, mod_ref, w_ref, gpost_ref, gffn_ref, x1_ref, h2_ref):
    ga1 = mod_ref[0, 2:3, :]
    sh2 = mod_ref[0, 3:4, :]
    sc2 = mod_ref[0, 4:5, :]
    tm, ka = oa_ref.shape
    chunks = [slice(r, r + PROJ_ROWS) for r in range(0, tm, PROJ_ROWS)]

    def project(rows):
        return (jnp.dot(oa_ref[rows, :], w_ref[:ka, :], preferred_element_type=F32)
                + jnp.dot(ob_ref[rows, :], w_ref[ka:, :], preferred_element_type=F32))

    t = project(chunks[0])
    for r, rows in enumerate(chunks):
        nxt = project(chunks[r + 1]) if r + 1 < len(chunks) else None
        x1 = x_ref[rows, :] + ga1 * (_rms(t) * gpost_ref[...])
        x1_ref[rows, :] = x1
        h2_ref[rows, :] = (_rms(x1) * gffn_ref[...] * (1.0 + sc2) + sh2).astype(BF16)
        t = nxt


def _proj(o_a, o_b, x2d, mods, w_out_bf, g_post, g_ffn, *, tiles_per_mod):
    t = x2d.shape[0]
    tm = 512
    nmod = mods.shape[0]
    mod_idx = (lambda i: (0, 0, 0)) if nmod == 1 else (lambda i: (i // tiles_per_mod, 0, 0))
    ka = D_MODEL // 2
    ob_idx = (lambda i: (i, 1)) if o_b.shape[1] == D_MODEL else (lambda i: (i, 0))
    return pl.pallas_call(
        _proj_kernel,
        grid=(t // tm,),
        in_specs=[
            pl.BlockSpec((tm, ka), lambda i: (i, 0)),
            pl.BlockSpec((tm, ka), ob_idx),
            pl.BlockSpec((tm, D_MODEL), lambda i: (i, 0)),
            pl.BlockSpec((1, 6, D_MODEL), mod_idx),
            pl.BlockSpec((D_MODEL, D_MODEL), lambda i: (0, 0), pipeline_mode=pl.Buffered(1)),
            pl.BlockSpec((1, D_MODEL), lambda i: (0, 0)),
            pl.BlockSpec((1, D_MODEL), lambda i: (0, 0)),
        ],
        out_specs=[
            pl.BlockSpec((tm, D_MODEL), lambda i: (i, 0)),
            pl.BlockSpec((tm, D_MODEL), lambda i: (i, 0)),
        ],
        out_shape=[
            jax.ShapeDtypeStruct((t, D_MODEL), F32),
            jax.ShapeDtypeStruct((t, D_MODEL), BF16),
        ],
        compiler_params=_cparams(("arbitrary",)),
        name="proj",
    )(o_a, o_b, x2d, mods, w_out_bf, g_post, g_ffn)


def _ffn_kernel(h2_ref, x1_ref, mod_ref, wv_ref, wg_ref, cwv_ref, cwg_ref, cbv_ref, cbg_ref,
                wd_ref, gpost_ref, out_ref, x1_buf, x1_sem, *u_scr, tm, seq):
    i = pl.program_id(0)
    j = pl.program_id(1)
    nseq = tm // seq
    stride = seq + 8

    def x1_copy():
        return pltpu.make_async_copy(x1_ref.at[pl.ds(pl.multiple_of(i * tm, tm), tm), :], x1_buf, x1_sem)

    @pl.when(j == 0)
    def _():
        x1_copy().start()
        out_ref[...] = jnp.zeros_like(out_ref)
        for scr in u_scr:
            for s in range(nseq + 1):
                scr[s * stride:s * stride + 8, :] = jnp.zeros((8, FFN_CW), F32)

    h2 = h2_ref[...]
    chunks = [slice(c * FFN_CW, (c + 1) * FFN_CW) for c in range(FFN_TF // FFN_CW)]
    for c, cols in enumerate(chunks):
        for k, w_ref in enumerate((wv_ref, wg_ref)):
            u = jnp.dot(h2, w_ref[:, cols], preferred_element_type=F32)
            for s in range(nseq):
                u_scr[2 * c + k][8 + s * stride:8 + s * stride + seq, :] = u[s * seq:(s + 1) * seq]

    def conv(scr, base, cw_ref, cb_ref):
        lo = scr[base - 1:base - 1 + seq, :]
        mid = scr[base:base + seq, :]
        hi = scr[base + 1:base + 1 + seq, :]
        return (mid * cw_ref[1:2, :] + cb_ref[...]) + lo * cw_ref[0:1, :] + hi * cw_ref[2:3, :]

    for c, cols in enumerate(chunks):
        acts = []
        for s in range(nseq):
            base = 8 + s * stride
            val = conv(u_scr[2 * c], base, cwv_ref.at[:, cols], cbv_ref.at[:, cols])
            gate = conv(u_scr[2 * c + 1], base, cwg_ref.at[:, cols], cbg_ref.at[:, cols])
            acts.append((gate / (1.0 + jnp.exp(-gate)) * val).astype(BF16))
        act = acts[0] if nseq == 1 else jnp.concatenate(acts, axis=0)
        out_ref[...] += jnp.dot(act, wd_ref[cols, :], preferred_element_type=F32)

    @pl.when(j == N_FFN_TILES - 1)
    def _():
        x1_copy().wait()
        ga2 = mod_ref[0, 5:6, :]
        out_ref[...] = x1_buf[...] + ga2 * (_rms(out_ref[...]) * gpost_ref[...])


def _ffn(h2, x1, mods, w_up_bf, conv_w, conv_b, w_down_bf, g_post, *, seq, tm):
    t = x1.shape[0]
    nmod = mods.shape[0]
    mod_idx = (lambda i, j: (0, 0, 0)) if nmod == 1 else (lambda i, j: (i * tm // seq, 0, 0))
    nj = N_FFN_TILES
    u_rows = 8 + (tm // seq) * (seq + 8)
    return pl.pallas_call(
        functools.partial(_ffn_kernel, tm=tm, seq=seq),
        grid=(t // tm, nj),
        in_specs=[
            pl.BlockSpec((tm, D_MODEL), lambda i, j: (i, 0), pipeline_mode=pl.Buffered(1)),
            pl.BlockSpec(memory_space=pl.ANY),
            pl.BlockSpec((1, 6, D_MODEL), mod_idx),
            pl.BlockSpec((D_MODEL, FFN_TF), lambda i, j: (0, j)),
            pl.BlockSpec((D_MODEL, FFN_TF), lambda i, j: (0, nj + j)),
            pl.BlockSpec((3, FFN_TF), lambda i, j: (0, j)),
            pl.BlockSpec((3, FFN_TF), lambda i, j: (0, nj + j)),
            pl.BlockSpec((1, FFN_TF), lambda i, j: (0, j)),
            pl.BlockSpec((1, FFN_TF), lambda i, j: (0, nj + j)),
            pl.BlockSpec((FFN_TF, D_MODEL), lambda i, j: (j, 0)),
            pl.BlockSpec((1, D_MODEL), lambda i, j: (0, 0)),
        ],
        out_specs=pl.BlockSpec((tm, D_MODEL), lambda i, j: (i, 0)),
        out_shape=jax.ShapeDtypeStruct((t, D_MODEL), F32),
        scratch_shapes=[pltpu.VMEM((tm, D_MODEL), F32), pltpu.SemaphoreType.DMA(())]
        + [pltpu.VMEM((u_rows, FFN_CW), F32) for _ in range(2 * (FFN_TF // FFN_CW))],
        compiler_params=_cparams(("arbitrary", "arbitrary")),
        name="ffn",
    )(h2, x1, mods, w_up_bf, w_up_bf, conv_w, conv_w, conv_b, conv_b, w_down_bf, g_post)


def _rope_tables(n_tok):
    half = HEAD_DIM // 2
    t = np.arange(n_tok)
    freqs = ROPE_THETA ** (-np.arange(0, half, 2, dtype=np.float64) / half)
    lane = np.arange(HEAD_DIM)
    pos = np.where(lane[None, :] < half, (t // GRID_W)[:, None], (t % GRID_W)[:, None])
    ang = pos * freqs[lane % (half // 2)][None, :]
    sign = np.where((lane % half) < half // 2, -1.0, 1.0)[None, :]
    return (jnp.asarray(np.cos(ang), F32), jnp.asarray(np.sin(ang) * sign, F32))


def kernel(x_prompt, x_sample, c, cache_a_k, cache_a_v, cache_b_k, cache_b_v, c_ctx, w_mod, b_mod,
           g_attn_pre, g_attn_post, g_ffn_pre, g_ffn_post, w_in, rpb, g_qnorm, g_knorm, w_out,
           w_up, conv_w, conv_b, w_down):
    batch, seq, _ = x_prompt.shape
    dec_batch, dec_seq, _ = x_sample.shape
    depth = w_mod.shape[0]
    assert depth == 1 and dec_seq == GRID_H * GRID_W

    xp = x_prompt.reshape(batch * seq, D_MODEL)
    xs = x_sample.reshape(dec_batch * dec_seq, D_MODEL)
    l = 0
    conds = jnp.concatenate(
        [c_ctx[None, :], c, jnp.zeros((16 - 1 - dec_batch, D_MODEL), F32)], axis=0)
    mods = _modulation(conds, w_mod[l], b_mod[l]).reshape(16, 6, D_MODEL)
    mod_ctx = mods[0:1]
    mod_lat = mods[1:1 + dec_batch]

    w_in_bf = w_in[l].astype(BF16)
    w_out_bf = w_out[l].astype(BF16)
    w_up_bf = w_up[l].astype(BF16)
    w_down_bf = w_down[l].astype(BF16)
    g_pre = g_attn_pre[l].reshape(1, D_MODEL)
    g_post = g_attn_post[l].reshape(1, D_MODEL)
    g_fpre = g_ffn_pre[l].reshape(1, D_MODEL)
    g_fpost = g_ffn_post[l].reshape(1, D_MODEL)
    g_q = g_qnorm[l].reshape(1, HEAD_DIM)
    g_k = g_knorm[l].reshape(1, HEAD_DIM)
    cb = conv_b[l].reshape(1, 2 * D_FF)

    qkvh_c, st_ak, st_av, st_bk, st_bv = _qkv(
        xp, mod_ctx, g_pre, w_in_bf, g_q, g_k, is_ctx=True, seq=seq)
    o_c = _attn_ctx(qkvh_c, seq=seq)
    x1_c, h2_c = _proj(o_c, o_c, xp, mod_ctx, w_out_bf, g_post, g_fpre, tiles_per_mod=1)
    y_c = _ffn(h2_c, x1_c, mod_ctx, w_up_bf, conv_w[l], cb, w_down_bf, g_fpost, seq=seq, tm=1024)

    qkvh_s = _qkv(xs, mod_lat, g_pre, w_in_bf, g_q, g_k, is_ctx=False, seq=dec_seq,
                  rope_tabs=_rope_tables(dec_seq))[0]
    bias = _bias_table(rpb[l])
    o_sa = _attn_lat_a(qkvh_s, cache_a_k, cache_a_v, bias)
    o_sb = _attn_lat_b(qkvh_s, cache_b_k, cache_b_v)
    x1_s, h2_s = _proj(o_sa, o_sb, xs, mod_lat, w_out_bf, g_post, g_fpre,
                       tiles_per_mod=dec_seq // 512)
    y_s = _ffn(h2_s, x1_s, mod_lat, w_up_bf, conv_w[l], cb, w_down_bf, g_fpost,
               seq=dec_seq, tm=1024)

    return (y_c.reshape(batch, seq, D_MODEL), y_s.reshape(dec_batch, dec_seq, D_MODEL),
            st_ak, st_av, st_bk, st_bv)
```

```python
import functools

import numpy as np
import jax
import jax.numpy as jnp
from jax import lax
from jax.experimental import pallas as pl
from jax.experimental.pallas import tpu as pltpu

F32 = jnp.float32
BF16 = jnp.bfloat16

D_MODEL = 2048
HEAD_DIM = 128
N_HEADS_A = 8
N_HEADS_B = 8
N_KV_B = 2
GRID_W = 64
GRID_H = 16
WIN_H = 8
WIN_W = 16
D_FF = 5632
ROPE_THETA = 10000.0
EPS = 1e-6
NEG_INF = -1e30
Q_SCALE = HEAD_DIM ** -0.5
IN_COLS = 4608
N_SLOTS = IN_COLS // HEAD_DIM
QKV_TN = 512
N_QKV_TILES = IN_COLS // QKV_TN
FFN_TF = 512
FFN_CW = 256
N_FFN_TILES = D_FF // FFN_TF
PROJ_ROWS = 128
HALF_Q = 512
BAND_K = 768
VMEM_LIMIT = 60 * 1024 * 1024


def _cparams(sem):
    return pltpu.CompilerParams(dimension_semantics=sem, vmem_limit_bytes=VMEM_LIMIT)


def _rms(x):
    return x * lax.rsqrt(jnp.mean(x * x, axis=-1, keepdims=True) + EPS)


def _mod_kernel(c_ref, w_ref, b_ref, o_ref):
    c = c_ref[...]
    a = (c / (1.0 + jnp.exp(-c))).astype(BF16)
    o_ref[...] = jnp.dot(a, w_ref[...].astype(BF16), preferred_element_type=F32) + b_ref[...]


def _modulation(conds, w_mod, b_mod):
    tn = 1024
    n = w_mod.shape[1]
    return pl.pallas_call(
        _mod_kernel,
        grid=(n // tn,),
        in_specs=[
            pl.BlockSpec((16, D_MODEL), lambda j: (0, 0)),
            pl.BlockSpec((D_MODEL, tn), lambda j: (0, j)),
            pl.BlockSpec((1, tn), lambda j: (0, j)),
        ],
        out_specs=pl.BlockSpec((16, tn), lambda j: (0, j)),
        out_shape=jax.ShapeDtypeStruct((16, n), F32),
        compiler_params=_cparams(("arbitrary",)),
        name="modulation",
    )(conds, w_mod, b_mod.reshape(1, n))


def _bias_kernel(rpb_ref, o_ref):
    h = pl.program_id(0)
    qc = lax.broadcasted_iota(jnp.int32, (GRID_W, 128), 0)
    lane = lax.broadcasted_iota(jnp.int32, (GRID_W, 128), 1)
    kc = lane & (GRID_W - 1)
    dmat = kc - qc + (WIN_W - 1)
    cs = jnp.clip(qc - WIN_W // 2, 0, GRID_W - WIN_W)
    col_valid = (kc >= cs) & (kc < cs + WIN_W)
    neg = jnp.full((GRID_W, 128), NEG_INF, F32)

    g = []
    for dr in range(2 * WIN_H - 1):
        acc = neg
        for d in range(2 * WIN_W - 1):
            acc = jnp.where(dmat == d, rpb_ref[h, dr * (2 * WIN_W - 1) + d], acc)
        g.append(jnp.where(col_valid, acc, neg))

    def row_tile(r, kr):
        rs = min(max(r - WIN_H // 2, 0), GRID_H - WIN_H)
        if rs <= kr < rs + WIN_H:
            return g[kr - r + WIN_H - 1]
        return neg

    for half in range(2):
        for lr in range(HALF_Q // GRID_W):
            r = 8 * half + lr
            for p in range(BAND_K // 128):
                kr0 = 4 * half + 2 * p
                tile = jnp.where(lane < GRID_W, row_tile(r, kr0), row_tile(r, kr0 + 1))
                o_ref[0, half, lr * GRID_W:(lr + 1) * GRID_W, p * 128:(p + 1) * 128] = tile


def _bias_table(rpb_l):
    rpb2 = rpb_l.reshape(N_HEADS_A, (2 * WIN_H - 1) * (2 * WIN_W - 1))
    return pl.pallas_call(
        _bias_kernel,
        grid=(N_HEADS_A,),
        in_specs=[pl.BlockSpec(memory_space=pltpu.SMEM)],
        out_specs=pl.BlockSpec((1, 2, HALF_Q, BAND_K), lambda h: (h, 0, 0, 0)),
        out_shape=jax.ShapeDtypeStruct((N_HEADS_A, 2, HALF_Q, BAND_K), F32),
        compiler_params=_cparams(("arbitrary",)),
        name="bias_table",
    )(rpb2)


def _rope(y, cos, sin_signed):
    lane = lax.broadcasted_iota(jnp.int32, y.shape, 1)
    swapped = jnp.where((lane & 63) < 32, pltpu.roll(y, 96, 1), pltpu.roll(y, 32, 1))
    return y * cos + swapped * sin_signed


def _qkv_kernel(*refs, is_ctx, tm, seq):
    if is_ctx:
        (x_ref, mod_ref, gpre_ref, w_ref, gq_ref, gk_ref,
         out_ref, sak_ref, sav_ref, sbk_ref, sbv_ref) = refs
    else:
        (x_ref, mod_ref, gpre_ref, w_ref, gq_ref, gk_ref, cos_ref, sin_ref, out_ref) = refs
    nb = max(tm // seq, 1)
    rows_per_state = min(seq, tm)
    heads_per_tile = QKV_TN // HEAD_DIM
    sh1 = mod_ref[0, 0:1, :]
    sc1 = mod_ref[0, 1:2, :]
    h = ((_rms(x_ref[...]) * gpre_ref[...]) * (1.0 + sc1) + sh1).astype(BF16)
    gq = gq_ref[...]
    gk = gk_ref[...]

    def project(n):
        return jnp.dot(h, w_ref[:, n * QKV_TN:(n + 1) * QKV_TN], preferred_element_type=F32)

    def store_state(s_ref, slot, val):
        for bb in range(nb):
            s_ref[bb, 0, slot] = val[bb * rows_per_state:(bb + 1) * rows_per_state]

    def rope(y):
        if is_ctx:
            return y
        return _rope(y, cos_ref[...], sin_ref[...])

    def epilogue(n, res):
        for hh in range(heads_per_tile):
            slot = n * heads_per_tile + hh
            v = res[:, hh * HEAD_DIM:(hh + 1) * HEAD_DIM]
            if slot < 8:
                out_ref[slot] = (v * Q_SCALE).astype(BF16)
            elif slot < 24:
                out_ref[slot] = v.astype(BF16)
                if is_ctx:
                    store_state(sak_ref if slot < 16 else sav_ref, slot % 8, v)
            elif slot < 32:
                out_ref[slot] = (rope(_rms(v) * gq) * Q_SCALE).astype(BF16)
            elif slot < 34:
                y = _rms(v) * gk
                if is_ctx:
                    store_state(sbk_ref, slot - 32, y)
                out_ref[slot] = rope(y).astype(BF16)
            else:
                out_ref[slot] = v.astype(BF16)
                if is_ctx:
                    store_state(sbv_ref, slot - 34, v)

    order = [6, 7, 8, 0, 1, 2, 3, 4, 5]
    res = project(order[0])
    for pos, n in enumerate(order):
        nxt = project(order[pos + 1]) if pos + 1 < len(order) else None
        epilogue(n, res)
        res = nxt


def _qkv(x2d, mods, g_pre, w_in_bf, g_q, g_k, *, is_ctx, seq, rope_tabs=None):
    t = x2d.shape[0]
    tm = 512
    nb = max(tm // seq, 1)
    tiles_per_seq = max(seq // tm, 1)
    nmod = mods.shape[0]
    mod_idx = (lambda i: (0, 0, 0)) if nmod == 1 else (lambda i: (i // tiles_per_seq, 0, 0))
    in_specs = [
        pl.BlockSpec((tm, D_MODEL), lambda i: (i, 0)),
        pl.BlockSpec((1, 6, D_MODEL), mod_idx),
        pl.BlockSpec((1, D_MODEL), lambda i: (0, 0)),
        pl.BlockSpec((D_MODEL, IN_COLS), lambda i: (0, 0), pipeline_mode=pl.Buffered(1)),
        pl.BlockSpec((1, HEAD_DIM), lambda i: (0, 0)),
        pl.BlockSpec((1, HEAD_DIM), lambda i: (0, 0)),
    ]
    args = [x2d, mods, g_pre, w_in_bf, g_q, g_k]
    out_specs = [pl.BlockSpec((N_SLOTS, tm, HEAD_DIM), lambda i: (0, i, 0))]
    out_shape = [jax.ShapeDtypeStruct((N_SLOTS, t, HEAD_DIM), BF16)]
    if is_ctx:
        assert tm % seq == 0
        nbatch = t // seq
        for nh in (N_HEADS_A, N_HEADS_A, N_KV_B, N_KV_B):
            out_specs.append(pl.BlockSpec((nb, 1, nh, seq, HEAD_DIM), lambda i: (i, 0, 0, 0, 0)))
            out_shape.append(jax.ShapeDtypeStruct((nbatch, 1, nh, seq, HEAD_DIM), F32))
    else:
        rope_idx = lambda i: (i % tiles_per_seq, 0)
        in_specs += [pl.BlockSpec((tm, HEAD_DIM), rope_idx), pl.BlockSpec((tm, HEAD_DIM), rope_idx)]
        args += list(rope_tabs)
    return pl.pallas_call(
        functools.partial(_qkv_kernel, is_ctx=is_ctx, tm=tm, seq=seq),
        grid=(t // tm,),
        in_specs=in_specs,
        out_specs=out_specs,
        out_shape=out_shape,
        compiler_params=_cparams(("arbitrary",)),
        name="qkv_ctx" if is_ctx else "qkv_lat",
    )(*args)


def _qk(q, k):
    return lax.dot_general(q, k, (((1,), (1,)), ((), ())), preferred_element_type=F32)


def _softmax_pv(scores, values):
    m = functools.reduce(jnp.maximum, [jnp.max(s, axis=-1, keepdims=True) for s in scores])
    acc = None
    den = None
    for s, v in zip(scores, values):
        e = jnp.exp(s - m)
        l = jnp.sum(e, axis=-1, keepdims=True)
        o = jnp.dot(e.astype(BF16), v, preferred_element_type=F32)
        acc = o if acc is None else acc + o
        den = l if den is None else den + l
    return acc / den


def _attn_ctx_kernel(qkv_ref, o_ref, *, nb, seq):
    for bb in range(nb):
        rows = slice(bb * seq, (bb + 1) * seq)
        for h in range(N_HEADS_A):
            q = qkv_ref[h, rows, :]
            k = qkv_ref[N_HEADS_A + h, rows, :]
            v = qkv_ref[2 * N_HEADS_A + h, rows, :]
            o = _softmax_pv([_qk(q, k)], [v])
            o_ref[rows, h * HEAD_DIM:(h + 1) * HEAD_DIM] = o.astype(BF16)
        group = N_HEADS_B // N_KV_B
        for j in range(N_KV_B):
            k = qkv_ref[32 + j, rows, :]
            v = qkv_ref[34 + j, rows, :]
            for g in range(group):
                hq = j * group + g
                q = qkv_ref[24 + hq, rows, :]
                o = _softmax_pv([_qk(q, k)], [v])
                c0 = (N_HEADS_A + hq) * HEAD_DIM
                o_ref[rows, c0:c0 + HEAD_DIM] = o.astype(BF16)


def _attn_ctx(qkvh, *, seq):
    t = qkvh.shape[1]
    nb = 2
    rows = nb * seq
    return pl.pallas_call(
        functools.partial(_attn_ctx_kernel, nb=nb, seq=seq),
        grid=(t // rows,),
        in_specs=[pl.BlockSpec((N_SLOTS, rows, HEAD_DIM), lambda i: (0, i, 0))],
        out_specs=pl.BlockSpec((rows, D_MODEL), lambda i: (i, 0)),
        out_shape=jax.ShapeDtypeStruct((t, D_MODEL), BF16),
        compiler_params=_cparams(("arbitrary",)),
        name="attn_ctx",
    )(qkvh)


def _attn_lat_a_kernel(q_ref, k0_ref, k1_ref, k2_ref, v0_ref, v1_ref, v2_ref,
                       ck_ref, cv_ref, bias_ref, o_ref):
    for h in range(N_HEADS_A):
        q = q_ref[h]
        s_nb = jnp.concatenate([_qk(q, k0_ref[h]), _qk(q, k1_ref[h]), _qk(q, k2_ref[h])], axis=1)
        s_nb = s_nb + bias_ref[h, 0]
        s_cx = _qk(q, ck_ref[0, 0, h].astype(BF16))
        v_nb = jnp.concatenate([v0_ref[h], v1_ref[h], v2_ref[h]], axis=0)
        o = _softmax_pv([s_nb, s_cx], [v_nb, cv_ref[0, 0, h].astype(BF16)])
        o_ref[:, h * HEAD_DIM:(h + 1) * HEAD_DIM] = o.astype(BF16)


def _attn_lat_a(qkvh, cache_k, cache_v, bias):
    t = qkvh.shape[1]
    nbatch = t // 1024
    kq = 256

    def kspec(slot_blk, j):
        return pl.BlockSpec((8, kq, HEAD_DIM), lambda hf, b: (slot_blk, b * 4 + hf + j, 0))

    cache_spec = pl.BlockSpec((1, 1, N_HEADS_A, 256, HEAD_DIM), lambda hf, b: (b, 0, 0, 0, 0))
    return pl.pallas_call(
        _attn_lat_a_kernel,
        grid=(2, nbatch),
        in_specs=[
            pl.BlockSpec((8, HALF_Q, HEAD_DIM), lambda hf, b: (0, b * 2 + hf, 0)),
            kspec(1, 0), kspec(1, 1), kspec(1, 2),
            kspec(2, 0), kspec(2, 1), kspec(2, 2),
            cache_spec, cache_spec,
            pl.BlockSpec((N_HEADS_A, 1, HALF_Q, BAND_K), lambda hf, b: (0, hf, 0, 0)),
        ],
        out_specs=pl.BlockSpec((HALF_Q, N_HEADS_A * HEAD_DIM), lambda hf, b: (b * 2 + hf, 0)),
        out_shape=jax.ShapeDtypeStruct((t, N_HEADS_A * HEAD_DIM), BF16),
        compiler_params=_cparams(("arbitrary", "arbitrary")),
        name="attn_lat_a",
    )(qkvh, qkvh, qkvh, qkvh, qkvh, qkvh, qkvh, cache_k, cache_v, bias)


def _attn_lat_b_kernel(q_ref, k_ref, v_ref, ck_ref, cv_ref, o_ref):
    k = k_ref[0]
    v = v_ref[0]
    ck = ck_ref[0, 0, 0].astype(BF16)
    cv = cv_ref[0, 0, 0].astype(BF16)
    for g in range(N_HEADS_B // N_KV_B):
        for qh in range(2):
            rows = slice(qh * 512, (qh + 1) * 512)
            q = q_ref[g, rows, :]
            o = _softmax_pv([_qk(q, k), _qk(q, ck)], [v, cv])
            o_ref[rows, g * HEAD_DIM:(g + 1) * HEAD_DIM] = o.astype(BF16)


def _attn_lat_b(qkvh, cache_k, cache_v):
    t = qkvh.shape[1]
    nbatch = t // 1024
    cache_spec = pl.BlockSpec((1, 1, 1, 256, HEAD_DIM), lambda b, j: (b, 0, j, 0, 0))
    return pl.pallas_call(
        _attn_lat_b_kernel,
        grid=(nbatch, N_KV_B),
        in_specs=[
            pl.BlockSpec((4, 1024, HEAD_DIM), lambda b, j: (6 + j, b, 0)),
            pl.BlockSpec((1, 1024, HEAD_DIM), lambda b, j: (32 + j, b, 0)),
            pl.BlockSpec((1, 1024, HEAD_DIM), lambda b, j: (34 + j, b, 0)),
            cache_spec, cache_spec,
        ],
        out_specs=pl.BlockSpec((1024, 512), lambda b, j: (b, j)),
        out_shape=jax.ShapeDtypeStruct((t, N_HEADS_B * HEAD_DIM), BF16),
        compiler_params=_cparams(("arbitrary", "arbitrary")),
        name="attn_lat_b",
    )(qkvh, qkvh, qkvh, cache_k, cache_v)


def _proj_kernel(oa_ref, ob_ref, x_ref, mod_ref, w_ref, gpost_ref, gffn_ref, x1_ref, h2_ref):
    ga1 = mod_ref[0, 2:3, :]
    sh2 = mod_ref[0, 3:4, :]
    sc2 = mod_ref[0, 4:5, :]
    tm, ka = oa_ref.shape
    chunks = [slice(r, r + PROJ_ROWS) for r in range(0, tm, PROJ_ROWS)]

    def project(rows):
        return (jnp.dot(oa_ref[rows, :], w_ref[:ka, :], preferred_element_type=F32)
                + jnp.dot(ob_ref[rows, :], w_ref[ka:, :], preferred_element_type=F32))

    t = project(chunks[0])
    for r, rows in enumerate(chunks):
        nxt = project(chunks[r + 1]) if r + 1 < len(chunks) else None
        x1 = x_ref[rows, :] + ga1 * (_rms(t) * gpost_ref[...])
        x1_ref[rows, :] = x1
        h2_ref[rows, :] = (_rms(x1) * gffn_ref[...] * (1.0 + sc2) + sh2).astype(BF16)
        t = nxt


def _proj(o_a, o_b, x2d, mods, w_out_bf, g_post, g_ffn, *, tiles_per_mod):
    t = x2d.shape[0]
    tm = 512
    nmod = mods.shape[0]
    mod_idx = (lambda i: (0, 0, 0)) if nmod == 1 else (lambda i: (i // tiles_per_mod, 0, 0))
    ka = D_MODEL // 2
    ob_idx = (lambda i: (i, 1)) if o_b.shape[1] == D_MODEL else (lambda i: (i, 0))
    return pl.pallas_call(
        _proj_kernel,
        grid=(t // tm,),
        in_specs=[
            pl.BlockSpec((tm, ka), lambda i: (i, 0)),
            pl.BlockSpec((tm, ka), ob_idx),
            pl.BlockSpec((tm, D_MODEL), lambda i: (i, 0)),
            pl.BlockSpec((1, 6, D_MODEL), mod_idx),
            pl.BlockSpec((D_MODEL, D_MODEL), lambda i: (0, 0), pipeline_mode=pl.Buffered(1)),
            pl.BlockSpec((1, D_MODEL), lambda i: (0, 0)),
            pl.BlockSpec((1, D_MODEL), lambda i: (0, 0)),
        ],
        out_specs=[
            pl.BlockSpec((tm, D_MODEL), lambda i: (i, 0)),
            pl.BlockSpec((tm, D_MODEL), lambda i: (i, 0)),
        ],
        out_shape=[
            jax.ShapeDtypeStruct((t, D_MODEL), F32),
            jax.ShapeDtypeStruct((t, D_MODEL), BF16),
        ],
        compiler_params=_cparams(("arbitrary",)),
        name="proj",
    )(o_a, o_b, x2d, mods, w_out_bf, g_post, g_ffn)


def _ffn_kernel(h2_ref, x1_ref, mod_ref, wv_ref, wg_ref, cpv_ref, cpg_ref,
                wd_ref, gpost_ref, out_ref, x1_buf, x1_sem, *u_scr, tm, seq):
    i = pl.program_id(0)
    j = pl.program_id(1)
    nseq = tm // seq
    stride = seq + 8
    chunks = [slice(c * FFN_CW, (c + 1) * FFN_CW) for c in range(FFN_TF // FFN_CW)]

    def x1_copy():
        return pltpu.make_async_copy(x1_ref.at[pl.ds(pl.multiple_of(i * tm, tm), tm), :], x1_buf, x1_sem)

    def conv(scr, base, cp_ref):
        lo = scr[base - 1:base - 1 + seq, :]
        mid = scr[base:base + seq, :]
        hi = scr[base + 1:base + 1 + seq, :]
        return (mid * cp_ref[1:2, :] + cp_ref[3:4, :]) + lo * cp_ref[0:1, :] + hi * cp_ref[2:3, :]

    def step(first):
        h2 = h2_ref[...]
        for c, cols in enumerate(chunks):
            for k, w_ref in enumerate((wv_ref, wg_ref)):
                u = jnp.dot(h2, w_ref[:, cols], preferred_element_type=F32)
                for s in range(nseq):
                    u_scr[2 * c + k][8 + s * stride:8 + s * stride + seq, :] = u[s * seq:(s + 1) * seq]
        for c, cols in enumerate(chunks):
            acts = []
            for s in range(nseq):
                base = 8 + s * stride
                val = conv(u_scr[2 * c], base, cpv_ref.at[:, cols])
                gate = conv(u_scr[2 * c + 1], base, cpg_ref.at[:, cols])
                acts.append((gate / (1.0 + jnp.exp(-gate)) * val).astype(BF16))
            act = acts[0] if nseq == 1 else jnp.concatenate(acts, axis=0)
            d = jnp.dot(act, wd_ref[cols, :], preferred_element_type=F32)
            if first and c == 0:
                out_ref[...] = d
            else:
                out_ref[...] += d

    @pl.when(j == 0)
    def _():
        x1_copy().start()
        for scr in u_scr:
            for s in range(nseq + 1):
                scr[s * stride:s * stride + 8, :] = jnp.zeros((8, FFN_CW), F32)
        step(True)

    @pl.when(j > 0)
    def _():
        step(False)

    @pl.when(j == N_FFN_TILES - 1)
    def _():
        x1_copy().wait()
        ga2 = mod_ref[0, 5:6, :]
        out_ref[...] = x1_buf[...] + ga2 * (_rms(out_ref[...]) * gpost_ref[...])


def _ffn(h2, x1, mods, w_up_bf, conv_p, w_down_bf, g_post, *, seq, tm):
    t = x1.shape[0]
    nmod = mods.shape[0]
    mod_idx = (lambda i, j: (0, 0, 0)) if nmod == 1 else (lambda i, j: (i * tm // seq, 0, 0))
    nj = N_FFN_TILES
    u_rows = 8 + (tm // seq) * (seq + 8)
    return pl.pallas_call(
        functools.partial(_ffn_kernel, tm=tm, seq=seq),
        grid=(t // tm, nj),
        in_specs=[
            pl.BlockSpec((tm, D_MODEL), lambda i, j: (i, 0)),
            pl.BlockSpec(memory_space=pl.ANY),
            pl.BlockSpec((1, 6, D_MODEL), mod_idx),
            pl.BlockSpec((D_MODEL, FFN_TF), lambda i, j: (0, j)),
            pl.BlockSpec((D_MODEL, FFN_TF), lambda i, j: (0, nj + j)),
            pl.BlockSpec((4, FFN_TF), lambda i, j: (0, j)),
            pl.BlockSpec((4, FFN_TF), lambda i, j: (0, nj + j)),
            pl.BlockSpec((FFN_TF, D_MODEL), lambda i, j: (j, 0)),
            pl.BlockSpec((1, D_MODEL), lambda i, j: (0, 0)),
        ],
        out_specs=pl.BlockSpec((tm, D_MODEL), lambda i, j: (i, 0)),
        out_shape=jax.ShapeDtypeStruct((t, D_MODEL), F32),
        scratch_shapes=[pltpu.VMEM((tm, D_MODEL), F32), pltpu.SemaphoreType.DMA(())]
        + [pltpu.VMEM((u_rows, FFN_CW), F32) for _ in range(2 * (FFN_TF // FFN_CW))],
        compiler_params=_cparams(("arbitrary", "arbitrary")),
        name="ffn",
    )(h2, x1, mods, w_up_bf, w_up_bf, conv_p, conv_p, w_down_bf, g_post)


def _rope_tables(n_tok):
    half = HEAD_DIM // 2
    t = np.arange(n_tok)
    freqs = ROPE_THETA ** (-np.arange(0, half, 2, dtype=np.float64) / half)
    lane = np.arange(HEAD_DIM)
    pos = np.where(lane[None, :] < half, (t // GRID_W)[:, None], (t % GRID_W)[:, None])
    ang = pos * freqs[lane % (half // 2)][None, :]
    sign = np.where((lane % half) < half // 2, -1.0, 1.0)[None, :]
    return (jnp.asarray(np.cos(ang), F32), jnp.asarray(np.sin(ang) * sign, F32))


def kernel(x_prompt, x_sample, c, cache_a_k, cache_a_v, cache_b_k, cache_b_v, c_ctx, w_mod, b_mod,
           g_attn_pre, g_attn_post, g_ffn_pre, g_ffn_post, w_in, rpb, g_qnorm, g_knorm, w_out,
           w_up, conv_w, conv_b, w_down):
    batch, seq, _ = x_prompt.shape
    dec_batch, dec_seq, _ = x_sample.shape
    depth = w_mod.shape[0]
    assert depth == 1 and dec_seq == GRID_H * GRID_W

    xp = x_prompt.reshape(batch * seq, D_MODEL)
    xs = x_sample.reshape(dec_batch * dec_seq, D_MODEL)
    l = 0
    conds = jnp.concatenate(
        [c_ctx[None, :], c, jnp.zeros((16 - 1 - dec_batch, D_MODEL), F32)], axis=0)
    mods = _modulation(conds, w_mod[l], b_mod[l]).reshape(16, 6, D_MODEL)
    mod_ctx = mods[0:1]
    mod_lat = mods[1:1 + dec_batch]

    w_in_bf = w_in[l].astype(BF16)
    w_out_bf = w_out[l].astype(BF16)
    w_up_bf = w_up[l].astype(BF16)
    w_down_bf = w_down[l].astype(BF16)
    g_pre = g_attn_pre[l].reshape(1, D_MODEL)
    g_post = g_attn_post[l].reshape(1, D_MODEL)
    g_fpre = g_ffn_pre[l].reshape(1, D_MODEL)
    g_fpost = g_ffn_post[l].reshape(1, D_MODEL)
    g_q = g_qnorm[l].reshape(1, HEAD_DIM)
    g_k = g_knorm[l].reshape(1, HEAD_DIM)
    conv_p = jnp.concatenate([conv_w[l], conv_b[l].reshape(1, 2 * D_FF)], axis=0)

    qkvh_c, st_ak, st_av, st_bk, st_bv = _qkv(
        xp, mod_ctx, g_pre, w_in_bf, g_q, g_k, is_ctx=True, seq=seq)
    o_c = _attn_ctx(qkvh_c, seq=seq)
    x1_c, h2_c = _proj(o_c, o_c, xp, mod_ctx, w_out_bf, g_post, g_fpre, tiles_per_mod=1)
    y_c = _ffn(h2_c, x1_c, mod_ctx, w_up_bf, conv_p, w_down_bf, g_fpost, seq=seq, tm=1024)

    qkvh_s = _qkv(xs, mod_lat, g_pre, w_in_bf, g_q, g_k, is_ctx=False, seq=dec_seq,
                  rope_tabs=_rope_tables(dec_seq))[0]
    bias = _bias_table(rpb[l])
    o_sa = _attn_lat_a(qkvh_s, cache_a_k, cache_a_v, bias)
    o_sb = _attn_lat_b(qkvh_s, cache_b_k, cache_b_v)
    x1_s, h2_s = _proj(o_sa, o_sb, xs, mod_lat, w_out_bf, g_post, g_fpre,
                       tiles_per_mod=dec_seq // 512)
    y_s = _ffn(h2_s, x1_s, mod_lat, w_up_bf, conv_p, w_down_bf, g_fpost,
               seq=dec_seq, tm=1024)

    return (y_c.reshape(batch, seq, D_MODEL), y_s.reshape(dec_batch, dec_seq, D_MODEL),
            st_ak, st_av, st_bk, st_bv)
```

```python
import functools

import numpy as np
import jax
import jax.numpy as jnp
from jax import lax
from jax.experimental import pallas as pl
from jax.experimental.pallas import tpu as pltpu

F32 = jnp.float32
BF16 = jnp.bfloat16

D_MODEL = 2048
HEAD_DIM = 128
N_HEADS_A = 8
N_HEADS_B = 8
N_KV_B = 2
GRID_W = 64
GRID_H = 16
WIN_H = 8
WIN_W = 16
D_FF = 5632
ROPE_THETA = 10000.0
EPS = 1e-6
NEG_INF = -1e30
Q_SCALE = HEAD_DIM ** -0.5
IN_COLS = 4608
N_SLOTS = IN_COLS // HEAD_DIM
QKV_TN = 512
N_QKV_TILES = IN_COLS // QKV_TN
FFN_TF = 512
FFN_CW = 256
N_FFN_TILES = D_FF // FFN_TF
FIN_ROWS = 128
PROJ_ROWS = 128
HALF_Q = 512
BAND_K = 768
VMEM_LIMIT = 60 * 1024 * 1024


def _cparams(sem):
    return pltpu.CompilerParams(dimension_semantics=sem, vmem_limit_bytes=VMEM_LIMIT)


def _rms(x):
    return x * lax.rsqrt(jnp.mean(x * x, axis=-1, keepdims=True) + EPS)


def _mod_kernel(c_ref, w_ref, b_ref, o_ref):
    c = c_ref[...]
    a = (c / (1.0 + jnp.exp(-c))).astype(BF16)
    o_ref[...] = jnp.dot(a, w_ref[...].astype(BF16), preferred_element_type=F32) + b_ref[...]


def _modulation(conds, w_mod, b_mod):
    tn = 1024
    n = w_mod.shape[1]
    return pl.pallas_call(
        _mod_kernel,
        grid=(n // tn,),
        in_specs=[
            pl.BlockSpec((16, D_MODEL), lambda j: (0, 0)),
            pl.BlockSpec((D_MODEL, tn), lambda j: (0, j)),
            pl.BlockSpec((1, tn), lambda j: (0, j)),
        ],
        out_specs=pl.BlockSpec((16, tn), lambda j: (0, j)),
        out_shape=jax.ShapeDtypeStruct((16, n), F32),
        compiler_params=_cparams(("arbitrary",)),
        name="modulation",
    )(conds, w_mod, b_mod.reshape(1, n))


def _bias_kernel(rpb_ref, o_ref):
    h = pl.program_id(0)
    qc = lax.broadcasted_iota(jnp.int32, (GRID_W, 128), 0)
    lane = lax.broadcasted_iota(jnp.int32, (GRID_W, 128), 1)
    kc = lane & (GRID_W - 1)
    dmat = kc - qc + (WIN_W - 1)
    cs = jnp.clip(qc - WIN_W // 2, 0, GRID_W - WIN_W)
    col_valid = (kc >= cs) & (kc < cs + WIN_W)
    neg = jnp.full((GRID_W, 128), NEG_INF, F32)

    g = []
    for dr in range(2 * WIN_H - 1):
        acc = neg
        for d in range(2 * WIN_W - 1):
            acc = jnp.where(dmat == d, rpb_ref[h, dr * (2 * WIN_W - 1) + d], acc)
        g.append(jnp.where(col_valid, acc, neg))

    def row_tile(r, kr):
        rs = min(max(r - WIN_H // 2, 0), GRID_H - WIN_H)
        if rs <= kr < rs + WIN_H:
            return g[kr - r + WIN_H - 1]
        return neg

    for half in range(2):
        for lr in range(HALF_Q // GRID_W):
            r = 8 * half + lr
            for p in range(BAND_K // 128):
                kr0 = 4 * half + 2 * p
                tile = jnp.where(lane < GRID_W, row_tile(r, kr0), row_tile(r, kr0 + 1))
                o_ref[0, half, lr * GRID_W:(lr + 1) * GRID_W, p * 128:(p + 1) * 128] = tile


def _bias_table(rpb_l):
    rpb2 = rpb_l.reshape(N_HEADS_A, (2 * WIN_H - 1) * (2 * WIN_W - 1))
    return pl.pallas_call(
        _bias_kernel,
        grid=(N_HEADS_A,),
        in_specs=[pl.BlockSpec(memory_space=pltpu.SMEM)],
        out_specs=pl.BlockSpec((1, 2, HALF_Q, BAND_K), lambda h: (h, 0, 0, 0)),
        out_shape=jax.ShapeDtypeStruct((N_HEADS_A, 2, HALF_Q, BAND_K), F32),
        compiler_params=_cparams(("arbitrary",)),
        name="bias_table",
    )(rpb2)


def _rope(y, cos, sin_signed):
    lane = lax.broadcasted_iota(jnp.int32, y.shape, 1)
    swapped = jnp.where((lane & 63) < 32, pltpu.roll(y, 96, 1), pltpu.roll(y, 32, 1))
    return y * cos + swapped * sin_signed


def _qkv_kernel(*refs, is_ctx, tm, seq):
    if is_ctx:
        (x_ref, mod_ref, gpre_ref, w_ref, gq_ref, gk_ref,
         out_ref, sak_ref, sav_ref, sbk_ref, sbv_ref) = refs
    else:
        (x_ref, mod_ref, gpre_ref, w_ref, gq_ref, gk_ref, cos_ref, sin_ref, out_ref) = refs
    nb = max(tm // seq, 1)
    rows_per_state = min(seq, tm)
    heads_per_tile = QKV_TN // HEAD_DIM
    sh1 = mod_ref[0, 0:1, :]
    sc1 = mod_ref[0, 1:2, :]
    h = (_rms(x_ref[...]) * (gpre_ref[...] * (1.0 + sc1)) + sh1).astype(BF16)
    gq = gq_ref[...]
    gk = gk_ref[...]

    def project(n):
        return jnp.dot(h, w_ref[:, n * QKV_TN:(n + 1) * QKV_TN], preferred_element_type=F32)

    def store_state(s_ref, slot, val):
        for bb in range(nb):
            s_ref[bb, 0, slot] = val[bb * rows_per_state:(bb + 1) * rows_per_state]

    def rope(y):
        if is_ctx:
            return y
        return _rope(y, cos_ref[...], sin_ref[...])

    def epilogue(n, res):
        for hh in range(heads_per_tile):
            slot = n * heads_per_tile + hh
            v = res[:, hh * HEAD_DIM:(hh + 1) * HEAD_DIM]
            if slot < 8:
                out_ref[slot] = (v * Q_SCALE).astype(BF16)
            elif slot < 24:
                out_ref[slot] = v.astype(BF16)
                if is_ctx:
                    store_state(sak_ref if slot < 16 else sav_ref, slot % 8, v)
            elif slot < 32:
                out_ref[slot] = (rope(_rms(v) * gq) * Q_SCALE).astype(BF16)
            elif slot < 34:
                y = _rms(v) * gk
                if is_ctx:
                    store_state(sbk_ref, slot - 32, y)
                out_ref[slot] = rope(y).astype(BF16)
            else:
                out_ref[slot] = v.astype(BF16)
                if is_ctx:
                    store_state(sbv_ref, slot - 34, v)

    order = [6, 7, 8, 0, 1, 2, 3, 4, 5]
    res = project(order[0])
    for pos, n in enumerate(order):
        nxt = project(order[pos + 1]) if pos + 1 < len(order) else None
        epilogue(n, res)
        res = nxt


def _qkv(x2d, mods, g_pre, w_in_bf, g_q, g_k, *, is_ctx, seq, rope_tabs=None):
    t = x2d.shape[0]
    tm = 512
    nb = max(tm // seq, 1)
    tiles_per_seq = max(seq // tm, 1)
    nmod = mods.shape[0]
    mod_idx = (lambda i: (0, 0, 0)) if nmod == 1 else (lambda i: (i // tiles_per_seq, 0, 0))
    in_specs = [
        pl.BlockSpec((tm, D_MODEL), lambda i: (i, 0)),
        pl.BlockSpec((1, 6, D_MODEL), mod_idx),
        pl.BlockSpec((1, D_MODEL), lambda i: (0, 0)),
        pl.BlockSpec((D_MODEL, IN_COLS), lambda i: (0, 0), pipeline_mode=pl.Buffered(1)),
        pl.BlockSpec((1, HEAD_DIM), lambda i: (0, 0)),
        pl.BlockSpec((1, HEAD_DIM), lambda i: (0, 0)),
    ]
    args = [x2d, mods, g_pre, w_in_bf, g_q, g_k]
    out_specs = [pl.BlockSpec((N_SLOTS, tm, HEAD_DIM), lambda i: (0, i, 0))]
    out_shape = [jax.ShapeDtypeStruct((N_SLOTS, t, HEAD_DIM), BF16)]
    if is_ctx:
        assert tm % seq == 0
        nbatch = t // seq
        for nh in (N_HEADS_A, N_HEADS_A, N_KV_B, N_KV_B):
            out_specs.append(pl.BlockSpec((nb, 1, nh, seq, HEAD_DIM), lambda i: (i, 0, 0, 0, 0)))
            out_shape.append(jax.ShapeDtypeStruct((nbatch, 1, nh, seq, HEAD_DIM), F32))
    else:
        rope_idx = lambda i: (i % tiles_per_seq, 0)
        in_specs += [pl.BlockSpec((tm, HEAD_DIM), rope_idx), pl.BlockSpec((tm, HEAD_DIM), rope_idx)]
        args += list(rope_tabs)
    return pl.pallas_call(
        functools.partial(_qkv_kernel, is_ctx=is_ctx, tm=tm, seq=seq),
        grid=(t // tm,),
        in_specs=in_specs,
        out_specs=out_specs,
        out_shape=out_shape,
        compiler_params=_cparams(("arbitrary",)),
        name="qkv_ctx" if is_ctx else "qkv_lat",
    )(*args)


def _qk(q, k):
    return lax.dot_general(q, k, (((1,), (1,)), ((), ())), preferred_element_type=F32)


def _with_ones(v):
    return jnp.concatenate([v, jnp.ones_like(v)], axis=1)


def _softmax_pv(scores, values):
    m = functools.reduce(jnp.maximum, [jnp.max(s, axis=-1, keepdims=True) for s in scores])
    with_ones = values[0].shape[1] == 2 * HEAD_DIM
    acc = None
    den = None
    for s, v in zip(scores, values):
        e = jnp.exp(s - m)
        o = jnp.dot(e.astype(BF16), v, preferred_element_type=F32)
        acc = o if acc is None else acc + o
        if not with_ones:
            l = jnp.sum(e, axis=-1, keepdims=True)
            den = l if den is None else den + l
    if with_ones:
        return acc[:, :HEAD_DIM] / acc[:, HEAD_DIM:HEAD_DIM + 1]
    return acc / den


def _attn_ctx_kernel(qkv_ref, o_ref, *, nb, seq):
    for bb in range(nb):
        rows = slice(bb * seq, (bb + 1) * seq)
        for h in range(N_HEADS_A):
            q = qkv_ref[h, rows, :]
            k = qkv_ref[N_HEADS_A + h, rows, :]
            v = qkv_ref[2 * N_HEADS_A + h, rows, :]
            o = _softmax_pv([_qk(q, k)], [v])
            o_ref[rows, h * HEAD_DIM:(h + 1) * HEAD_DIM] = o.astype(BF16)
        group = N_HEADS_B // N_KV_B
        for j in range(N_KV_B):
            k = qkv_ref[32 + j, rows, :]
            v = qkv_ref[34 + j, rows, :]
            q = jnp.concatenate([qkv_ref[24 + j * group + g, rows, :] for g in range(group)], axis=0)
            o = _softmax_pv([_qk(q, k)], [v])
            for g in range(group):
                c0 = (N_HEADS_A + j * group + g) * HEAD_DIM
                o_ref[rows, c0:c0 + HEAD_DIM] = o[g * seq:(g + 1) * seq].astype(BF16)


def _attn_ctx(qkvh, *, seq):
    t = qkvh.shape[1]
    nb = 2
    rows = nb * seq
    return pl.pallas_call(
        functools.partial(_attn_ctx_kernel, nb=nb, seq=seq),
        grid=(t // rows,),
        in_specs=[pl.BlockSpec((N_SLOTS, rows, HEAD_DIM), lambda i: (0, i, 0))],
        out_specs=pl.BlockSpec((rows, D_MODEL), lambda i: (i, 0)),
        out_shape=jax.ShapeDtypeStruct((t, D_MODEL), BF16),
        compiler_params=_cparams(("arbitrary",)),
        name="attn_ctx",
    )(qkvh)


def _attn_lat_a_kernel(q_ref, k0_ref, k1_ref, k2_ref, v0_ref, v1_ref, v2_ref,
                       ck_ref, cv_ref, bias_ref, o_ref):
    for h in range(N_HEADS_A):
        q = q_ref[h]
        s_nb = jnp.concatenate([_qk(q, k0_ref[h]), _qk(q, k1_ref[h]), _qk(q, k2_ref[h])], axis=1)
        s_nb = s_nb + bias_ref[h, 0]
        s_cx = _qk(q, ck_ref[0, 0, h].astype(BF16))
        v_nb = jnp.concatenate([v0_ref[h], v1_ref[h], v2_ref[h]], axis=0)
        o = _softmax_pv([s_nb, s_cx], [_with_ones(v_nb), _with_ones(cv_ref[0, 0, h].astype(BF16))])
        o_ref[:, h * HEAD_DIM:(h + 1) * HEAD_DIM] = o.astype(BF16)


def _attn_lat_a(qkvh, cache_k, cache_v, bias):
    t = qkvh.shape[1]
    nbatch = t // 1024
    kq = 256

    def kspec(slot_blk, j):
        return pl.BlockSpec((8, kq, HEAD_DIM), lambda hf, b: (slot_blk, b * 4 + hf + j, 0))

    cache_spec = pl.BlockSpec((1, 1, N_HEADS_A, 256, HEAD_DIM), lambda hf, b: (b, 0, 0, 0, 0))
    return pl.pallas_call(
        _attn_lat_a_kernel,
        grid=(2, nbatch),
        in_specs=[
            pl.BlockSpec((8, HALF_Q, HEAD_DIM), lambda hf, b: (0, b * 2 + hf, 0)),
            kspec(1, 0), kspec(1, 1), kspec(1, 2),
            kspec(2, 0), kspec(2, 1), kspec(2, 2),
            cache_spec, cache_spec,
            pl.BlockSpec((N_HEADS_A, 1, HALF_Q, BAND_K), lambda hf, b: (0, hf, 0, 0)),
        ],
        out_specs=pl.BlockSpec((HALF_Q, N_HEADS_A * HEAD_DIM), lambda hf, b: (b * 2 + hf, 0)),
        out_shape=jax.ShapeDtypeStruct((t, N_HEADS_A * HEAD_DIM), BF16),
        compiler_params=_cparams(("arbitrary", "arbitrary")),
        name="attn_lat_a",
    )(qkvh, qkvh, qkvh, qkvh, qkvh, qkvh, qkvh, cache_k, cache_v, bias)


def _attn_lat_b_kernel(q_ref, k_ref, v_ref, ck_ref, cv_ref, o_ref):
    k = k_ref[0]
    v = _with_ones(v_ref[0])
    ck = ck_ref[0, 0, 0].astype(BF16)
    cv = _with_ones(cv_ref[0, 0, 0].astype(BF16))
    for g in range(N_HEADS_B // N_KV_B):
        for qh in range(2):
            rows = slice(qh * 512, (qh + 1) * 512)
            q = q_ref[g, rows, :]
            o = _softmax_pv([_qk(q, k), _qk(q, ck)], [v, cv])
            o_ref[rows, g * HEAD_DIM:(g + 1) * HEAD_DIM] = o.astype(BF16)


def _attn_lat_b(qkvh, cache_k, cache_v):
    t = qkvh.shape[1]
    nbatch = t // 1024
    cache_spec = pl.BlockSpec((1, 1, 1, 256, HEAD_DIM), lambda b, j: (b, 0, j, 0, 0))
    return pl.pallas_call(
        _attn_lat_b_kernel,
        grid=(nbatch, N_KV_B),
        in_specs=[
            pl.BlockSpec((4, 1024, HEAD_DIM), lambda b, j: (6 + j, b, 0)),
            pl.BlockSpec((1, 1024, HEAD_DIM), lambda b, j: (32 + j, b, 0)),
            pl.BlockSpec((1, 1024, HEAD_DIM), lambda b, j: (34 + j, b, 0)),
            cache_spec, cache_spec,
        ],
        out_specs=pl.BlockSpec((1024, 512), lambda b, j: (b, j)),
        out_shape=jax.ShapeDtypeStruct((t, N_HEADS_B * HEAD_DIM), BF16),
        compiler_params=_cparams(("arbitrary", "arbitrary")),
        name="attn_lat_b",
    )(qkvh, qkvh, qkvh, cache_k, cache_v)


def _proj_kernel(oa_ref, ob_ref, x_ref, mod_ref, w_ref, gpost_ref, gffn_ref, x1_ref, h2_ref):
    gain1 = mod_ref[0, 2:3, :] * gpost_ref[...]
    sh2 = mod_ref[0, 3:4, :]
    gain2 = gffn_ref[...] * (1.0 + mod_ref[0, 4:5, :])
    tm, ka = oa_ref.shape
    chunks = [slice(r, r + PROJ_ROWS) for r in range(0, tm, PROJ_ROWS)]

    def project(rows):
        return (jnp.dot(oa_ref[rows, :], w_ref[:ka, :], preferred_element_type=F32)
                + jnp.dot(ob_ref[rows, :], w_ref[ka:, :], preferred_element_type=F32))

    t = project(chunks[0])
    for r, rows in enumerate(chunks):
        nxt = project(chunks[r + 1]) if r + 1 < len(chunks) else None
        x1 = x_ref[rows, :] + gain1 * _rms(t)
        x1_ref[rows, :] = x1
        h2_ref[rows, :] = (_rms(x1) * gain2 + sh2).astype(BF16)
        t = nxt


def _proj(o_a, o_b, x2d, mods, w_out_bf, g_post, g_ffn, *, tiles_per_mod):
    t = x2d.shape[0]
    tm = 512
    nmod = mods.shape[0]
    mod_idx = (lambda i: (0, 0, 0)) if nmod == 1 else (lambda i: (i // tiles_per_mod, 0, 0))
    ka = D_MODEL // 2
    ob_idx = (lambda i: (i, 1)) if o_b.shape[1] == D_MODEL else (lambda i: (i, 0))
    return pl.pallas_call(
        _proj_kernel,
        grid=(t // tm,),
        in_specs=[
            pl.BlockSpec((tm, ka), lambda i: (i, 0)),
            pl.BlockSpec((tm, ka), ob_idx),
            pl.BlockSpec((tm, D_MODEL), lambda i: (i, 0)),
            pl.BlockSpec((1, 6, D_MODEL), mod_idx),
            pl.BlockSpec((D_MODEL, D_MODEL), lambda i: (0, 0), pipeline_mode=pl.Buffered(1)),
            pl.BlockSpec((1, D_MODEL), lambda i: (0, 0)),
            pl.BlockSpec((1, D_MODEL), lambda i: (0, 0)),
        ],
        out_specs=[
            pl.BlockSpec((tm, D_MODEL), lambda i: (i, 0)),
            pl.BlockSpec((tm, D_MODEL), lambda i: (i, 0)),
        ],
        out_shape=[
            jax.ShapeDtypeStruct((t, D_MODEL), F32),
            jax.ShapeDtypeStruct((t, D_MODEL), BF16),
        ],
        compiler_params=_cparams(("arbitrary",)),
        name="proj",
    )(o_a, o_b, x2d, mods, w_out_bf, g_post, g_ffn)


def _ffn_kernel(h2_ref, x1_ref, mod_ref, wv_ref, wg_ref, cpv_ref, cpg_ref,
                wd_ref, gpost_ref, out_ref, x1_buf, x1_sem, *u_scr, tm, seq):
    i = pl.program_id(0)
    j = pl.program_id(1)
    nseq = tm // seq
    stride = seq + 8
    chunks = [slice(c * FFN_CW, (c + 1) * FFN_CW) for c in range(FFN_TF // FFN_CW)]

    def x1_copy():
        return pltpu.make_async_copy(x1_ref.at[pl.ds(pl.multiple_of(i * tm, tm), tm), :], x1_buf, x1_sem)

    def conv(scr, base, cp_ref):
        lo = scr[base - 1:base - 1 + seq, :]
        mid = scr[base:base + seq, :]
        hi = scr[base + 1:base + 1 + seq, :]
        return (mid * cp_ref[1:2, :] + cp_ref[3:4, :]) + lo * cp_ref[0:1, :] + hi * cp_ref[2:3, :]

    def step(first):
        h2 = h2_ref[...]
        for c, cols in enumerate(chunks):
            for k, w_ref in enumerate((wv_ref, wg_ref)):
                u = jnp.dot(h2, w_ref[:, cols], preferred_element_type=F32)
                for s in range(nseq):
                    u_scr[2 * c + k][8 + s * stride:8 + s * stride + seq, :] = u[s * seq:(s + 1) * seq]
        for c, cols in enumerate(chunks):
            acts = []
            for s in range(nseq):
                base = 8 + s * stride
                val = conv(u_scr[2 * c], base, cpv_ref.at[:, cols])
                gate = conv(u_scr[2 * c + 1], base, cpg_ref.at[:, cols])
                acts.append((gate / (1.0 + jnp.exp(-gate)) * val).astype(BF16))
            act = acts[0] if nseq == 1 else jnp.concatenate(acts, axis=0)
            d = jnp.dot(act, wd_ref[cols, :], preferred_element_type=F32)
            if first and c == 0:
                out_ref[...] = d
            else:
                out_ref[...] += d

    @pl.when(j == 0)
    def _():
        x1_copy().start()
        for scr in u_scr:
            for s in range(nseq + 1):
                scr[s * stride:s * stride + 8, :] = jnp.zeros((8, FFN_CW), F32)
        step(True)

    @pl.when(j > 0)
    def _():
        step(False)

    @pl.when(j == N_FFN_TILES - 1)
    def _():
        x1_copy().wait()
        gain = mod_ref[0, 5:6, :] * gpost_ref[...]
        for r in range(0, tm, FIN_ROWS):
            rows = slice(r, r + FIN_ROWS)
            out_ref[rows, :] = x1_buf[rows, :] + gain * _rms(out_ref[rows, :])


def _ffn(h2, x1, mods, w_up_bf, conv_p, w_down_bf, g_post, *, seq, tm):
    t = x1.shape[0]
    nmod = mods.shape[0]
    mod_idx = (lambda i, j: (0, 0, 0)) if nmod == 1 else (lambda i, j: (i * tm // seq, 0, 0))
    nj = N_FFN_TILES
    u_rows = 8 + (tm // seq) * (seq + 8)
    return pl.pallas_call(
        functools.partial(_ffn_kernel, tm=tm, seq=seq),
        grid=(t // tm, nj),
        in_specs=[
            pl.BlockSpec((tm, D_MODEL), lambda i, j: (i, 0)),
            pl.BlockSpec(memory_space=pl.ANY),
            pl.BlockSpec((1, 6, D_MODEL), mod_idx),
            pl.BlockSpec((D_MODEL, FFN_TF), lambda i, j: (0, j)),
            pl.BlockSpec((D_MODEL, FFN_TF), lambda i, j: (0, nj + j)),
            pl.BlockSpec((4, FFN_TF), lambda i, j: (0, j)),
            pl.BlockSpec((4, FFN_TF), lambda i, j: (0, nj + j)),
            pl.BlockSpec((FFN_TF, D_MODEL), lambda i, j: (j, 0)),
            pl.BlockSpec((1, D_MODEL), lambda i, j: (0, 0)),
        ],
        out_specs=pl.BlockSpec((tm, D_MODEL), lambda i, j: (i, 0)),
        out_shape=jax.ShapeDtypeStruct((t, D_MODEL), F32),
        scratch_shapes=[pltpu.VMEM((tm, D_MODEL), F32), pltpu.SemaphoreType.DMA(())]
        + [pltpu.VMEM((u_rows, FFN_CW), F32) for _ in range(2 * (FFN_TF // FFN_CW))],
        compiler_params=_cparams(("arbitrary", "arbitrary")),
        name="ffn",
    )(h2, x1, mods, w_up_bf, w_up_bf, conv_p, conv_p, w_down_bf, g_post)


def _rope_tables(n_tok):
    half = HEAD_DIM // 2
    t = np.arange(n_tok)
    freqs = ROPE_THETA ** (-np.arange(0, half, 2, dtype=np.float64) / half)
    lane = np.arange(HEAD_DIM)
    pos = np.where(lane[None, :] < half, (t // GRID_W)[:, None], (t % GRID_W)[:, None])
    ang = pos * freqs[lane % (half // 2)][None, :]
    sign = np.where((lane % half) < half // 2, -1.0, 1.0)[None, :]
    return (jnp.asarray(np.cos(ang), F32), jnp.asarray(np.sin(ang) * sign, F32))


def kernel(x_prompt, x_sample, c, cache_a_k, cache_a_v, cache_b_k, cache_b_v, c_ctx, w_mod, b_mod,
           g_attn_pre, g_attn_post, g_ffn_pre, g_ffn_post, w_in, rpb, g_qnorm, g_knorm, w_out,
           w_up, conv_w, conv_b, w_down):
    batch, seq, _ = x_prompt.shape
    dec_batch, dec_seq, _ = x_sample.shape
    depth = w_mod.shape[0]
    assert depth == 1 and dec_seq == GRID_H * GRID_W

    xp = x_prompt.reshape(batch * seq, D_MODEL)
    xs = x_sample.reshape(dec_batch * dec_seq, D_MODEL)
    l = 0
    conds = jnp.concatenate(
        [c_ctx[None, :], c, jnp.zeros((16 - 1 - dec_batch, D_MODEL), F32)], axis=0)
    mods = _modulation(conds, w_mod[l], b_mod[l]).reshape(16, 6, D_MODEL)
    mod_ctx = mods[0:1]
    mod_lat = mods[1:1 + dec_batch]

    w_in_bf = w_in[l].astype(BF16)
    w_out_bf = w_out[l].astype(BF16)
    w_up_bf = w_up[l].astype(BF16)
    w_down_bf = w_down[l].astype(BF16)
    g_pre = g_attn_pre[l].reshape(1, D_MODEL)
    g_post = g_attn_post[l].reshape(1, D_MODEL)
    g_fpre = g_ffn_pre[l].reshape(1, D_MODEL)
    g_fpost = g_ffn_post[l].reshape(1, D_MODEL)
    g_q = g_qnorm[l].reshape(1, HEAD_DIM)
    g_k = g_knorm[l].reshape(1, HEAD_DIM)
    conv_p = jnp.concatenate([conv_w[l], conv_b[l].reshape(1, 2 * D_FF)], axis=0)

    qkvh_c, st_ak, st_av, st_bk, st_bv = _qkv(
        xp, mod_ctx, g_pre, w_in_bf, g_q, g_k, is_ctx=True, seq=seq)
    o_c = _attn_ctx(qkvh_c, seq=seq)
    x1_c, h2_c = _proj(o_c, o_c, xp, mod_ctx, w_out_bf, g_post, g_fpre, tiles_per_mod=1)
    y_c = _ffn(h2_c, x1_c, mod_ctx, w_up_bf, conv_p, w_down_bf, g_fpost, seq=seq, tm=1024)

    qkvh_s = _qkv(xs, mod_lat, g_pre, w_in_bf, g_q, g_k, is_ctx=False, seq=dec_seq,
                  rope_tabs=_rope_tables(dec_seq))[0]
    bias = _bias_table(rpb[l])
    o_sa = _attn_lat_a(qkvh_s, cache_a_k, cache_a_v, bias)
    o_sb = _attn_lat_b(qkvh_s, cache_b_k, cache_b_v)
    x1_s, h2_s = _proj(o_sa, o_sb, xs, mod_lat, w_out_bf, g_post, g_fpre,
                       tiles_per_mod=dec_seq // 512)
    y_s = _ffn(h2_s, x1_s, mod_lat, w_up_bf, conv_p, w_down_bf, g_fpost,
               seq=dec_seq, tm=1024)

    return (y_c.reshape(batch, seq, D_MODEL), y_s.reshape(dec_batch, dec_seq, D_MODEL),
            st_ak, st_av, st_bk, st_bv)
```

```python
import functools

import numpy as np
import jax
import jax.numpy as jnp
from jax import lax
from jax.experimental import pallas as pl
from jax.experimental.pallas import tpu as pltpu

F32 = jnp.float32
BF16 = jnp.bfloat16

D_MODEL = 2048
HEAD_DIM = 128
N_HEADS_A = 8
N_HEADS_B = 8
N_KV_B = 2
GRID_W = 64
GRID_H = 16
WIN_H = 8
WIN_W = 16
D_FF = 5632
ROPE_THETA = 10000.0
EPS = 1e-6
NEG_INF = -1e30
Q_SCALE = HEAD_DIM ** -0.5
IN_COLS = 4608
N_SLOTS = IN_COLS // HEAD_DIM
QKV_TN = 512
N_QKV_TILES = IN_COLS // QKV_TN
FFN_TF = 512
FFN_CW = 256
N_FFN_TILES = D_FF // FFN_TF
FIN_ROWS = 128
PROJ_ROWS = 128
HALF_Q = 512
BAND_K = 768
VMEM_LIMIT = 60 * 1024 * 1024


def _cparams(sem):
    return pltpu.CompilerParams(dimension_semantics=sem, vmem_limit_bytes=VMEM_LIMIT)


def _rms(x):
    return x * lax.rsqrt(jnp.mean(x * x, axis=-1, keepdims=True) + EPS)


def _mod_kernel(c_ref, w_ref, b_ref, o_ref):
    c = c_ref[...]
    a = (c / (1.0 + jnp.exp(-c))).astype(BF16)
    o_ref[...] = jnp.dot(a, w_ref[...].astype(BF16), preferred_element_type=F32) + b_ref[...]


def _modulation(conds, w_mod, b_mod):
    tn = 1024
    n = w_mod.shape[1]
    return pl.pallas_call(
        _mod_kernel,
        grid=(n // tn,),
        in_specs=[
            pl.BlockSpec((16, D_MODEL), lambda j: (0, 0)),
            pl.BlockSpec((D_MODEL, tn), lambda j: (0, j)),
            pl.BlockSpec((1, tn), lambda j: (0, j)),
        ],
        out_specs=pl.BlockSpec((16, tn), lambda j: (0, j)),
        out_shape=jax.ShapeDtypeStruct((16, n), F32),
        compiler_params=_cparams(("arbitrary",)),
        name="modulation",
    )(conds, w_mod, b_mod.reshape(1, n))


def _bias_kernel(rpb_ref, o_ref):
    h = pl.program_id(0)
    qc = lax.broadcasted_iota(jnp.int32, (GRID_W, 128), 0)
    lane = lax.broadcasted_iota(jnp.int32, (GRID_W, 128), 1)
    kc = lane & (GRID_W - 1)
    dmat = kc - qc + (WIN_W - 1)
    cs = jnp.clip(qc - WIN_W // 2, 0, GRID_W - WIN_W)
    col_valid = (kc >= cs) & (kc < cs + WIN_W)
    neg = jnp.full((GRID_W, 128), NEG_INF, F32)

    g = []
    for dr in range(2 * WIN_H - 1):
        acc = neg
        for d in range(2 * WIN_W - 1):
            acc = jnp.where(dmat == d, rpb_ref[h, dr * (2 * WIN_W - 1) + d], acc)
        g.append(jnp.where(col_valid, acc, neg))

    def row_tile(r, kr):
        rs = min(max(r - WIN_H // 2, 0), GRID_H - WIN_H)
        if rs <= kr < rs + WIN_H:
            return g[kr - r + WIN_H - 1]
        return neg

    for half in range(2):
        for lr in range(HALF_Q // GRID_W):
            r = 8 * half + lr
            for p in range(BAND_K // 128):
                kr0 = 4 * half + 2 * p
                tile = jnp.where(lane < GRID_W, row_tile(r, kr0), row_tile(r, kr0 + 1))
                o_ref[0, half, lr * GRID_W:(lr + 1) * GRID_W, p * 128:(p + 1) * 128] = tile


def _bias_table(rpb_l):
    rpb2 = rpb_l.reshape(N_HEADS_A, (2 * WIN_H - 1) * (2 * WIN_W - 1))
    return pl.pallas_call(
        _bias_kernel,
        grid=(N_HEADS_A,),
        in_specs=[pl.BlockSpec(memory_space=pltpu.SMEM)],
        out_specs=pl.BlockSpec((1, 2, HALF_Q, BAND_K), lambda h: (h, 0, 0, 0)),
        out_shape=jax.ShapeDtypeStruct((N_HEADS_A, 2, HALF_Q, BAND_K), F32),
        compiler_params=_cparams(("arbitrary",)),
        name="bias_table",
    )(rpb2)


def _rope(y, cos, sin_signed):
    lane = lax.broadcasted_iota(jnp.int32, y.shape, 1)
    swapped = jnp.where((lane & 63) < 32, pltpu.roll(y, 96, 1), pltpu.roll(y, 32, 1))
    return y * cos + swapped * sin_signed


def _qkv_kernel(*refs, is_ctx, tm, seq):
    if is_ctx:
        (x_ref, mod_ref, gpre_ref, w_ref, gq_ref, gk_ref, wf_ref,
         out_ref, sak_ref, sav_ref, sbk_ref, sbv_ref, wb_ref) = refs
        wb_ref[...] = wf_ref[...].astype(BF16)
    else:
        (x_ref, mod_ref, gpre_ref, w_ref, gq_ref, gk_ref, cos_ref, sin_ref, out_ref) = refs
    nb = max(tm // seq, 1)
    rows_per_state = min(seq, tm)
    heads_per_tile = QKV_TN // HEAD_DIM
    sh1 = mod_ref[0, 0:1, :]
    sc1 = mod_ref[0, 1:2, :]
    h = (_rms(x_ref[...]) * (gpre_ref[...] * (1.0 + sc1)) + sh1).astype(BF16)
    gq = gq_ref[...]
    gk = gk_ref[...]

    def project(n):
        return jnp.dot(h, w_ref[:, n * QKV_TN:(n + 1) * QKV_TN], preferred_element_type=F32)

    def store_state(s_ref, slot, val):
        for bb in range(nb):
            s_ref[bb, 0, slot] = val[bb * rows_per_state:(bb + 1) * rows_per_state]

    def rope(y):
        if is_ctx:
            return y
        return _rope(y, cos_ref[...], sin_ref[...])

    def epilogue(n, res):
        for hh in range(heads_per_tile):
            slot = n * heads_per_tile + hh
            v = res[:, hh * HEAD_DIM:(hh + 1) * HEAD_DIM]
            if slot < 8:
                out_ref[slot] = (v * Q_SCALE).astype(BF16)
            elif slot < 24:
                out_ref[slot] = v.astype(BF16)
                if is_ctx:
                    store_state(sak_ref if slot < 16 else sav_ref, slot % 8, v)
            elif slot < 32:
                out_ref[slot] = (rope(_rms(v) * gq) * Q_SCALE).astype(BF16)
            elif slot < 34:
                y = _rms(v) * gk
                if is_ctx:
                    store_state(sbk_ref, slot - 32, y)
                out_ref[slot] = rope(y).astype(BF16)
            else:
                out_ref[slot] = v.astype(BF16)
                if is_ctx:
                    store_state(sbv_ref, slot - 34, v)

    order = [6, 7, 8, 0, 1, 2, 3, 4, 5]
    res = project(order[0])
    for pos, n in enumerate(order):
        nxt = project(order[pos + 1]) if pos + 1 < len(order) else None
        epilogue(n, res)
        res = nxt


def _qkv(x2d, mods, g_pre, w_in_bf, g_q, g_k, *, is_ctx, seq, rope_tabs=None, cast_w=None):
    t = x2d.shape[0]
    tm = 512
    nb = max(tm // seq, 1)
    tiles_per_seq = max(seq // tm, 1)
    nmod = mods.shape[0]
    mod_idx = (lambda i: (0, 0, 0)) if nmod == 1 else (lambda i: (i // tiles_per_seq, 0, 0))
    in_specs = [
        pl.BlockSpec((tm, D_MODEL), lambda i: (i, 0)),
        pl.BlockSpec((1, 6, D_MODEL), mod_idx),
        pl.BlockSpec((1, D_MODEL), lambda i: (0, 0)),
        pl.BlockSpec((D_MODEL, IN_COLS), lambda i: (0, 0), pipeline_mode=pl.Buffered(1)),
        pl.BlockSpec((1, HEAD_DIM), lambda i: (0, 0)),
        pl.BlockSpec((1, HEAD_DIM), lambda i: (0, 0)),
    ]
    args = [x2d, mods, g_pre, w_in_bf, g_q, g_k]
    out_specs = [pl.BlockSpec((N_SLOTS, tm, HEAD_DIM), lambda i: (0, i, 0))]
    out_shape = [jax.ShapeDtypeStruct((N_SLOTS, t, HEAD_DIM), BF16)]
    if is_ctx:
        assert tm % seq == 0
        nbatch = t // seq
        for nh in (N_HEADS_A, N_HEADS_A, N_KV_B, N_KV_B):
            out_specs.append(pl.BlockSpec((nb, 1, nh, seq, HEAD_DIM), lambda i: (i, 0, 0, 0, 0)))
            out_shape.append(jax.ShapeDtypeStruct((nbatch, 1, nh, seq, HEAD_DIM), F32))
        w_in_spec, w_out_spec, w_shape = _cast_specs(cast_w, t // tm)
        in_specs.append(w_in_spec)
        args.append(cast_w)
        out_specs.append(w_out_spec)
        out_shape.append(w_shape)
    else:
        rope_idx = lambda i: (i % tiles_per_seq, 0)
        in_specs += [pl.BlockSpec((tm, HEAD_DIM), rope_idx), pl.BlockSpec((tm, HEAD_DIM), rope_idx)]
        args += list(rope_tabs)
    return pl.pallas_call(
        functools.partial(_qkv_kernel, is_ctx=is_ctx, tm=tm, seq=seq),
        grid=(t // tm,),
        in_specs=in_specs,
        out_specs=out_specs,
        out_shape=out_shape,
        compiler_params=_cparams(("arbitrary",)),
        name="qkv_ctx" if is_ctx else "qkv_lat",
    )(*args)


def _qk(q, k):
    return lax.dot_general(q, k, (((1,), (1,)), ((), ())), preferred_element_type=F32)


def _with_ones(v):
    return jnp.concatenate([v, jnp.ones_like(v)], axis=1)


def _softmax_pv(scores, values):
    m = functools.reduce(jnp.maximum, [jnp.max(s, axis=-1, keepdims=True) for s in scores])
    with_ones = values[0].shape[1] == 2 * HEAD_DIM
    acc = None
    den = None
    for s, v in zip(scores, values):
        e = jnp.exp(s - m)
        o = jnp.dot(e.astype(BF16), v, preferred_element_type=F32)
        acc = o if acc is None else acc + o
        if not with_ones:
            l = jnp.sum(e, axis=-1, keepdims=True)
            den = l if den is None else den + l
    if with_ones:
        return acc[:, :HEAD_DIM] / acc[:, HEAD_DIM:HEAD_DIM + 1]
    return acc / den


def _cast_specs(w, steps):
    rows = w.shape[0] // steps
    assert rows * steps == w.shape[0] and rows % 16 == 0
    spec = pl.BlockSpec((rows, w.shape[1]), lambda i: (i, 0))
    return spec, spec, jax.ShapeDtypeStruct(w.shape, BF16)


def _attn_ctx_kernel(qkv_ref, wf_ref, o_ref, wb_ref, *, nb, seq):
    wb_ref[...] = wf_ref[...].astype(BF16)
    for bb in range(nb):
        rows = slice(bb * seq, (bb + 1) * seq)
        for h in range(N_HEADS_A):
            q = qkv_ref[h, rows, :]
            k = qkv_ref[N_HEADS_A + h, rows, :]
            v = qkv_ref[2 * N_HEADS_A + h, rows, :]
            o = _softmax_pv([_qk(q, k)], [v])
            o_ref[rows, h * HEAD_DIM:(h + 1) * HEAD_DIM] = o.astype(BF16)
        group = N_HEADS_B // N_KV_B
        for j in range(N_KV_B):
            k = qkv_ref[32 + j, rows, :]
            v = qkv_ref[34 + j, rows, :]
            q = jnp.concatenate([qkv_ref[24 + j * group + g, rows, :] for g in range(group)], axis=0)
            o = _softmax_pv([_qk(q, k)], [v])
            for g in range(group):
                c0 = (N_HEADS_A + j * group + g) * HEAD_DIM
                o_ref[rows, c0:c0 + HEAD_DIM] = o[g * seq:(g + 1) * seq].astype(BF16)


def _attn_ctx(qkvh, w_f32, *, seq):
    t = qkvh.shape[1]
    nb = 2
    rows = nb * seq
    steps = t // rows
    w_in_spec, w_out_spec, w_shape = _cast_specs(w_f32, steps)
    return pl.pallas_call(
        functools.partial(_attn_ctx_kernel, nb=nb, seq=seq),
        grid=(steps,),
        in_specs=[pl.BlockSpec((N_SLOTS, rows, HEAD_DIM), lambda i: (0, i, 0)), w_in_spec],
        out_specs=[pl.BlockSpec((rows, D_MODEL), lambda i: (i, 0)), w_out_spec],
        out_shape=[jax.ShapeDtypeStruct((t, D_MODEL), BF16), w_shape],
        compiler_params=_cparams(("arbitrary",)),
        name="attn_ctx",
    )(qkvh, w_f32)


def _attn_lat_a_kernel(q_ref, k0_ref, k1_ref, k2_ref, v0_ref, v1_ref, v2_ref,
                       ck_ref, cv_ref, bias_ref, o_ref):
    for h in range(N_HEADS_A):
        q = q_ref[h]
        s_nb = jnp.concatenate([_qk(q, k0_ref[h]), _qk(q, k1_ref[h]), _qk(q, k2_ref[h])], axis=1)
        s_nb = s_nb + bias_ref[h, 0]
        s_cx = _qk(q, ck_ref[0, 0, h].astype(BF16))
        v_nb = jnp.concatenate([v0_ref[h], v1_ref[h], v2_ref[h]], axis=0)
        o = _softmax_pv([s_nb, s_cx], [_with_ones(v_nb), _with_ones(cv_ref[0, 0, h].astype(BF16))])
        o_ref[:, h * HEAD_DIM:(h + 1) * HEAD_DIM] = o.astype(BF16)


def _attn_lat_a(qkvh, cache_k, cache_v, bias):
    t = qkvh.shape[1]
    nbatch = t // 1024
    kq = 256

    def kspec(slot_blk, j):
        return pl.BlockSpec((8, kq, HEAD_DIM), lambda hf, b: (slot_blk, b * 4 + hf + j, 0))

    cache_spec = pl.BlockSpec((1, 1, N_HEADS_A, 256, HEAD_DIM), lambda hf, b: (b, 0, 0, 0, 0))
    return pl.pallas_call(
        _attn_lat_a_kernel,
        grid=(2, nbatch),
        in_specs=[
            pl.BlockSpec((8, HALF_Q, HEAD_DIM), lambda hf, b: (0, b * 2 + hf, 0)),
            kspec(1, 0), kspec(1, 1), kspec(1, 2),
            kspec(2, 0), kspec(2, 1), kspec(2, 2),
            cache_spec, cache_spec,
            pl.BlockSpec((N_HEADS_A, 1, HALF_Q, BAND_K), lambda hf, b: (0, hf, 0, 0)),
        ],
        out_specs=pl.BlockSpec((HALF_Q, N_HEADS_A * HEAD_DIM), lambda hf, b: (b * 2 + hf, 0)),
        out_shape=jax.ShapeDtypeStruct((t, N_HEADS_A * HEAD_DIM), BF16),
        compiler_params=_cparams(("arbitrary", "arbitrary")),
        name="attn_lat_a",
    )(qkvh, qkvh, qkvh, qkvh, qkvh, qkvh, qkvh, cache_k, cache_v, bias)


def _attn_lat_b_kernel(q_ref, k_ref, v_ref, ck_ref, cv_ref, o_ref):
    k = k_ref[0]
    v = _with_ones(v_ref[0])
    ck = ck_ref[0, 0, 0].astype(BF16)
    cv = _with_ones(cv_ref[0, 0, 0].astype(BF16))
    for g in range(N_HEADS_B // N_KV_B):
        for qh in range(2):
            rows = slice(qh * 512, (qh + 1) * 512)
            q = q_ref[g, rows, :]
            o = _softmax_pv([_qk(q, k), _qk(q, ck)], [v, cv])
            o_ref[rows, g * HEAD_DIM:(g + 1) * HEAD_DIM] = o.astype(BF16)


def _attn_lat_b(qkvh, cache_k, cache_v):
    t = qkvh.shape[1]
    nbatch = t // 1024
    cache_spec = pl.BlockSpec((1, 1, 1, 256, HEAD_DIM), lambda b, j: (b, 0, j, 0, 0))
    return pl.pallas_call(
        _attn_lat_b_kernel,
        grid=(nbatch, N_KV_B),
        in_specs=[
            pl.BlockSpec((4, 1024, HEAD_DIM), lambda b, j: (6 + j, b, 0)),
            pl.BlockSpec((1, 1024, HEAD_DIM), lambda b, j: (32 + j, b, 0)),
            pl.BlockSpec((1, 1024, HEAD_DIM), lambda b, j: (34 + j, b, 0)),
            cache_spec, cache_spec,
        ],
        out_specs=pl.BlockSpec((1024, 512), lambda b, j: (b, j)),
        out_shape=jax.ShapeDtypeStruct((t, N_HEADS_B * HEAD_DIM), BF16),
        compiler_params=_cparams(("arbitrary", "arbitrary")),
        name="attn_lat_b",
    )(qkvh, qkvh, qkvh, cache_k, cache_v)


def _proj_kernel(oa_ref, ob_ref, x_ref, mod_ref, w_ref, gpost_ref, gffn_ref, *rest):
    if len(rest) == 4:
        wf_ref, x1_ref, h2_ref, wb_ref = rest
        wb_ref[...] = wf_ref[...].astype(BF16)
    else:
        x1_ref, h2_ref = rest
    gain1 = mod_ref[0, 2:3, :] * gpost_ref[...]
    sh2 = mod_ref[0, 3:4, :]
    gain2 = gffn_ref[...] * (1.0 + mod_ref[0, 4:5, :])
    tm, ka = oa_ref.shape
    chunks = [slice(r, r + PROJ_ROWS) for r in range(0, tm, PROJ_ROWS)]

    def project(rows):
        return (jnp.dot(oa_ref[rows, :], w_ref[:ka, :], preferred_element_type=F32)
                + jnp.dot(ob_ref[rows, :], w_ref[ka:, :], preferred_element_type=F32))

    t = project(chunks[0])
    for r, rows in enumerate(chunks):
        nxt = project(chunks[r + 1]) if r + 1 < len(chunks) else None
        x1 = x_ref[rows, :] + gain1 * _rms(t)
        x1_ref[rows, :] = x1
        h2_ref[rows, :] = (_rms(x1) * gain2 + sh2).astype(BF16)
        t = nxt


def _proj(o_a, o_b, x2d, mods, w_out_bf, g_post, g_ffn, *, tiles_per_mod, cast_w=None):
    t = x2d.shape[0]
    tm = 512
    nmod = mods.shape[0]
    mod_idx = (lambda i: (0, 0, 0)) if nmod == 1 else (lambda i: (i // tiles_per_mod, 0, 0))
    ka = D_MODEL // 2
    ob_idx = (lambda i: (i, 1)) if o_b.shape[1] == D_MODEL else (lambda i: (i, 0))
    in_specs = [
        pl.BlockSpec((tm, ka), lambda i: (i, 0)),
        pl.BlockSpec((tm, ka), ob_idx),
        pl.BlockSpec((tm, D_MODEL), lambda i: (i, 0)),
        pl.BlockSpec((1, 6, D_MODEL), mod_idx),
        pl.BlockSpec((D_MODEL, D_MODEL), lambda i: (0, 0), pipeline_mode=pl.Buffered(1)),
        pl.BlockSpec((1, D_MODEL), lambda i: (0, 0)),
        pl.BlockSpec((1, D_MODEL), lambda i: (0, 0)),
    ]
    out_specs = [
        pl.BlockSpec((tm, D_MODEL), lambda i: (i, 0)),
        pl.BlockSpec((tm, D_MODEL), lambda i: (i, 0)),
    ]
    out_shape = [
        jax.ShapeDtypeStruct((t, D_MODEL), F32),
        jax.ShapeDtypeStruct((t, D_MODEL), BF16),
    ]
    args = [o_a, o_b, x2d, mods, w_out_bf, g_post, g_ffn]
    if cast_w is not None:
        w_in_spec, w_out_spec, w_shape = _cast_specs(cast_w, t // tm)
        in_specs.append(w_in_spec)
        out_specs.append(w_out_spec)
        out_shape.append(w_shape)
        args.append(cast_w)
    return pl.pallas_call(
        _proj_kernel,
        grid=(t // tm,),
        in_specs=in_specs,
        out_specs=out_specs,
        out_shape=out_shape,
        compiler_params=_cparams(("arbitrary",)),
        name="proj",
    )(*args)


def _ffn_kernel(h2_ref, x1_ref, mod_ref, wv_ref, wg_ref, cpv_ref, cpg_ref,
                wd_ref, gpost_ref, out_ref, x1_buf, x1_sem, *u_scr, tm, seq):
    i = pl.program_id(0)
    j = pl.program_id(1)
    nseq = tm // seq
    stride = seq + 8
    chunks = [slice(c * FFN_CW, (c + 1) * FFN_CW) for c in range(FFN_TF // FFN_CW)]

    def x1_copy():
        return pltpu.make_async_copy(x1_ref.at[pl.ds(pl.multiple_of(i * tm, tm), tm), :], x1_buf, x1_sem)

    def conv(scr, base, cp_ref):
        lo = scr[base - 1:base - 1 + seq, :]
        mid = scr[base:base + seq, :]
        hi = scr[base + 1:base + 1 + seq, :]
        return (mid * cp_ref[1:2, :] + cp_ref[3:4, :]) + lo * cp_ref[0:1, :] + hi * cp_ref[2:3, :]

    def step(first):
        h2 = h2_ref[...]
        for c, cols in enumerate(chunks):
            for k, w_ref in enumerate((wv_ref, wg_ref)):
                u = jnp.dot(h2, w_ref[:, cols], preferred_element_type=F32)
                for s in range(nseq):
                    u_scr[2 * c + k][8 + s * stride:8 + s * stride + seq, :] = u[s * seq:(s + 1) * seq]
        for c, cols in enumerate(chunks):
            acts = []
            for s in range(nseq):
                base = 8 + s * stride
                val = conv(u_scr[2 * c], base, cpv_ref.at[:, cols])
                gate = conv(u_scr[2 * c + 1], base, cpg_ref.at[:, cols])
                acts.append((gate / (1.0 + jnp.exp(-gate)) * val).astype(BF16))
            act = acts[0] if nseq == 1 else jnp.concatenate(acts, axis=0)
            d = jnp.dot(act, wd_ref[cols, :], preferred_element_type=F32)
            if first and c == 0:
                out_ref[...] = d
            else:
                out_ref[...] += d

    @pl.when(j == 0)
    def _():
        x1_copy().start()
        for scr in u_scr:
            for s in range(nseq + 1):
                scr[s * stride:s * stride + 8, :] = jnp.zeros((8, FFN_CW), F32)
        step(True)

    @pl.when(j > 0)
    def _():
        step(False)

    @pl.when(j == N_FFN_TILES - 1)
    def _():
        x1_copy().wait()
        gain = mod_ref[0, 5:6, :] * gpost_ref[...]
        for r in range(0, tm, FIN_ROWS):
            rows = slice(r, r + FIN_ROWS)
            out_ref[rows, :] = x1_buf[rows, :] + gain * _rms(out_ref[rows, :])


def _ffn(h2, x1, mods, w_up_bf, conv_p, w_down_bf, g_post, *, seq, tm):
    t = x1.shape[0]
    nmod = mods.shape[0]
    mod_idx = (lambda i, j: (0, 0, 0)) if nmod == 1 else (lambda i, j: (i * tm // seq, 0, 0))
    nj = N_FFN_TILES
    u_rows = 8 + (tm // seq) * (seq + 8)
    return pl.pallas_call(
        functools.partial(_ffn_kernel, tm=tm, seq=seq),
        grid=(t // tm, nj),
        in_specs=[
            pl.BlockSpec((tm, D_MODEL), lambda i, j: (i, 0)),
            pl.BlockSpec(memory_space=pl.ANY),
            pl.BlockSpec((1, 6, D_MODEL), mod_idx),
            pl.BlockSpec((D_MODEL, FFN_TF), lambda i, j: (0, j)),
            pl.BlockSpec((D_MODEL, FFN_TF), lambda i, j: (0, nj + j)),
            pl.BlockSpec((4, FFN_TF), lambda i, j: (0, j)),
            pl.BlockSpec((4, FFN_TF), lambda i, j: (0, nj + j)),
            pl.BlockSpec((FFN_TF, D_MODEL), lambda i, j: (j, 0)),
            pl.BlockSpec((1, D_MODEL), lambda i, j: (0, 0)),
        ],
        out_specs=pl.BlockSpec((tm, D_MODEL), lambda i, j: (i, 0)),
        out_shape=jax.ShapeDtypeStruct((t, D_MODEL), F32),
        scratch_shapes=[pltpu.VMEM((tm, D_MODEL), F32), pltpu.SemaphoreType.DMA(())]
        + [pltpu.VMEM((u_rows, FFN_CW), F32) for _ in range(2 * (FFN_TF // FFN_CW))],
        compiler_params=_cparams(("arbitrary", "arbitrary")),
        name="ffn",
    )(h2, x1, mods, w_up_bf, w_up_bf, conv_p, conv_p, w_down_bf, g_post)


def _rope_tables(n_tok):
    half = HEAD_DIM // 2
    t = np.arange(n_tok)
    freqs = ROPE_THETA ** (-np.arange(0, half, 2, dtype=np.float64) / half)
    lane = np.arange(HEAD_DIM)
    pos = np.where(lane[None, :] < half, (t // GRID_W)[:, None], (t % GRID_W)[:, None])
    ang = pos * freqs[lane % (half // 2)][None, :]
    sign = np.where((lane % half) < half // 2, -1.0, 1.0)[None, :]
    return (jnp.asarray(np.cos(ang), F32), jnp.asarray(np.sin(ang) * sign, F32))


def kernel(x_prompt, x_sample, c, cache_a_k, cache_a_v, cache_b_k, cache_b_v, c_ctx, w_mod, b_mod,
           g_attn_pre, g_attn_post, g_ffn_pre, g_ffn_post, w_in, rpb, g_qnorm, g_knorm, w_out,
           w_up, conv_w, conv_b, w_down):
    batch, seq, _ = x_prompt.shape
    dec_batch, dec_seq, _ = x_sample.shape
    depth = w_mod.shape[0]
    assert depth == 1 and dec_seq == GRID_H * GRID_W

    xp = x_prompt.reshape(batch * seq, D_MODEL)
    xs = x_sample.reshape(dec_batch * dec_seq, D_MODEL)
    l = 0
    conds = jnp.concatenate(
        [c_ctx[None, :], c, jnp.zeros((16 - 1 - dec_batch, D_MODEL), F32)], axis=0)
    mods = _modulation(conds, w_mod[l], b_mod[l]).reshape(16, 6, D_MODEL)
    mod_ctx = mods[0:1]
    mod_lat = mods[1:1 + dec_batch]

    w_in_bf = w_in[l].astype(BF16)
    g_pre = g_attn_pre[l].reshape(1, D_MODEL)
    g_post = g_attn_post[l].reshape(1, D_MODEL)
    g_fpre = g_ffn_pre[l].reshape(1, D_MODEL)
    g_fpost = g_ffn_post[l].reshape(1, D_MODEL)
    g_q = g_qnorm[l].reshape(1, HEAD_DIM)
    g_k = g_knorm[l].reshape(1, HEAD_DIM)
    conv_p = jnp.concatenate([conv_w[l], conv_b[l].reshape(1, 2 * D_FF)], axis=0)

    qkvh_c, st_ak, st_av, st_bk, st_bv, w_out_bf = _qkv(
        xp, mod_ctx, g_pre, w_in_bf, g_q, g_k, is_ctx=True, seq=seq, cast_w=w_out[l])
    o_c, w_up_bf = _attn_ctx(qkvh_c, w_up[l], seq=seq)
    x1_c, h2_c, w_down_bf = _proj(o_c, o_c, xp, mod_ctx, w_out_bf, g_post, g_fpre, tiles_per_mod=1,
                                  cast_w=w_down[l])
    y_c = _ffn(h2_c, x1_c, mod_ctx, w_up_bf, conv_p, w_down_bf, g_fpost, seq=seq, tm=1024)

    qkvh_s = _qkv(xs, mod_lat, g_pre, w_in_bf, g_q, g_k, is_ctx=False, seq=dec_seq,
                  rope_tabs=_rope_tables(dec_seq))[0]
    bias = _bias_table(rpb[l])
    o_sa = _attn_lat_a(qkvh_s, cache_a_k, cache_a_v, bias)
    o_sb = _attn_lat_b(qkvh_s, cache_b_k, cache_b_v)
    x1_s, h2_s = _proj(o_sa, o_sb, xs, mod_lat, w_out_bf, g_post, g_fpre,
                       tiles_per_mod=dec_seq // 512)
    y_s = _ffn(h2_s, x1_s, mod_lat, w_up_bf, conv_p, w_down_bf, g_fpost,
               seq=dec_seq, tm=1024)

    return (y_c.reshape(batch, seq, D_MODEL), y_s.reshape(dec_batch, dec_seq, D_MODEL),
            st_ak, st_av, st_bk, st_bv)
```

```python
import functools

import numpy as np
import jax
import jax.numpy as jnp
from jax import lax
from jax.experimental import pallas as pl
from jax.experimental.pallas import tpu as pltpu

F32 = jnp.float32
BF16 = jnp.bfloat16

D_MODEL = 2048
HEAD_DIM = 128
N_HEADS_A = 8
N_HEADS_B = 8
N_KV_B = 2
GRID_W = 64
GRID_H = 16
WIN_H = 8
WIN_W = 16
D_FF = 5632
ROPE_THETA = 10000.0
EPS = 1e-6
NEG_INF = -1e30
Q_SCALE = HEAD_DIM ** -0.5
IN_COLS = 4608
N_SLOTS = IN_COLS // HEAD_DIM
QKV_TN = 512
N_QKV_TILES = IN_COLS // QKV_TN
FFN_TF = 512
FFN_CW = 256
N_FFN_TILES = D_FF // FFN_TF
FIN_ROWS = 128
PROJ_ROWS = 128
ATT_ROWS = 128
HALF_Q = 512
BAND_K = 768
VMEM_LIMIT = 60 * 1024 * 1024


def _cparams(sem):
    return pltpu.CompilerParams(dimension_semantics=sem, vmem_limit_bytes=VMEM_LIMIT)


def _rms(x):
    return x * lax.rsqrt(jnp.mean(x * x, axis=-1, keepdims=True) + EPS)


def _mod_kernel(c_ref, w_ref, b_ref, o_ref):
    c = c_ref[...]
    a = (c / (1.0 + jnp.exp(-c))).astype(BF16)
    o_ref[...] = jnp.dot(a, w_ref[...].astype(BF16), preferred_element_type=F32) + b_ref[...]


def _modulation(conds, w_mod, b_mod):
    tn = 1024
    n = w_mod.shape[1]
    return pl.pallas_call(
        _mod_kernel,
        grid=(n // tn,),
        in_specs=[
            pl.BlockSpec((16, D_MODEL), lambda j: (0, 0)),
            pl.BlockSpec((D_MODEL, tn), lambda j: (0, j)),
            pl.BlockSpec((1, tn), lambda j: (0, j)),
        ],
        out_specs=pl.BlockSpec((16, tn), lambda j: (0, j)),
        out_shape=jax.ShapeDtypeStruct((16, n), F32),
        compiler_params=_cparams(("arbitrary",)),
        name="modulation",
    )(conds, w_mod, b_mod.reshape(1, n))


def _bias_kernel(rpb_ref, wf_ref, o_ref, wb_ref):
    _cast_slabs([wf_ref], [wb_ref])
    h = pl.program_id(0)
    qc = lax.broadcasted_iota(jnp.int32, (GRID_W, 128), 0)
    lane = lax.broadcasted_iota(jnp.int32, (GRID_W, 128), 1)
    kc = lane & (GRID_W - 1)
    dmat = kc - qc + (WIN_W - 1)
    cs = jnp.clip(qc - WIN_W // 2, 0, GRID_W - WIN_W)
    col_valid = (kc >= cs) & (kc < cs + WIN_W)
    neg = jnp.full((GRID_W, 128), NEG_INF, F32)

    g = []
    for dr in range(2 * WIN_H - 1):
        acc = neg
        for d in range(2 * WIN_W - 1):
            acc = jnp.where(dmat == d, rpb_ref[h, dr * (2 * WIN_W - 1) + d], acc)
        g.append(jnp.where(col_valid, acc, neg))

    def row_tile(r, kr):
        rs = min(max(r - WIN_H // 2, 0), GRID_H - WIN_H)
        if rs <= kr < rs + WIN_H:
            return g[kr - r + WIN_H - 1]
        return neg

    for half in range(2):
        for lr in range(HALF_Q // GRID_W):
            r = 8 * half + lr
            for p in range(BAND_K // 128):
                kr0 = 4 * half + 2 * p
                tile = jnp.where(lane < GRID_W, row_tile(r, kr0), row_tile(r, kr0 + 1))
                o_ref[0, half, lr * GRID_W:(lr + 1) * GRID_W, p * 128:(p + 1) * 128] = tile


def _bias_table(rpb_l, cast_w):
    rpb2 = rpb_l.reshape(N_HEADS_A, (2 * WIN_H - 1) * (2 * WIN_W - 1))
    w_in_spec, w_out_spec, w_shape = _cast_specs(cast_w, N_HEADS_A)
    return pl.pallas_call(
        _bias_kernel,
        grid=(N_HEADS_A,),
        in_specs=[pl.BlockSpec(memory_space=pltpu.SMEM), w_in_spec],
        out_specs=[pl.BlockSpec((1, 2, HALF_Q, BAND_K), lambda h: (h, 0, 0, 0)), w_out_spec],
        out_shape=[jax.ShapeDtypeStruct((N_HEADS_A, 2, HALF_Q, BAND_K), F32), w_shape],
        compiler_params=_cparams(("arbitrary",)),
        name="bias_table",
    )(rpb2, cast_w)


def _cast_specs(w, steps, *, col_block=(0, 1), grid_index=lambda i: i):
    k, n = col_block
    rows = w.shape[0] // steps
    cols = w.shape[1] // n
    assert rows * steps == w.shape[0] and rows % 16 == 0 and cols * n == w.shape[1]
    in_spec = pl.BlockSpec((rows, cols), lambda *g: (grid_index(*g), k))
    out_spec = pl.BlockSpec((rows, cols), lambda *g: (grid_index(*g), 0))
    return in_spec, out_spec, jax.ShapeDtypeStruct((w.shape[0], cols), BF16)


def _cast_slabs(cast_in, cast_out):
    for wf_ref, wb_ref in zip(cast_in, cast_out):
        wb_ref[...] = wf_ref[...].astype(BF16)


def _rope(y, cos, sin_signed):
    lane = lax.broadcasted_iota(jnp.int32, y.shape, 1)
    swapped = jnp.where((lane & 63) < 32, pltpu.roll(y, 96, 1), pltpu.roll(y, 32, 1))
    return y * cos + swapped * sin_signed


def _qkv_kernel(*refs, is_ctx, tm, seq, n_cast):
    n_in = (6 if is_ctx else 8) + n_cast
    _cast_slabs(refs[n_in - n_cast:n_in], refs[len(refs) - n_cast:])
    if is_ctx:
        x_ref, mod_ref, gpre_ref, w_ref, gq_ref, gk_ref = refs[:6]
        out_ref, sak_ref, sav_ref, sbk_ref, sbv_ref = refs[n_in:n_in + 5]
    else:
        x_ref, mod_ref, gpre_ref, w_ref, gq_ref, gk_ref, cos_ref, sin_ref = refs[:8]
        out_ref = refs[n_in]
    nb = max(tm // seq, 1)
    rows_per_state = min(seq, tm)
    heads_per_tile = QKV_TN // HEAD_DIM
    sh1 = mod_ref[0, 0:1, :]
    sc1 = mod_ref[0, 1:2, :]
    h = (_rms(x_ref[...]) * (gpre_ref[...] * (1.0 + sc1)) + sh1).astype(BF16)
    gq = gq_ref[...]
    gk = gk_ref[...]

    def project(n):
        return jnp.dot(h, w_ref[:, n * QKV_TN:(n + 1) * QKV_TN], preferred_element_type=F32)

    def store_state(s_ref, slot, val):
        for bb in range(nb):
            s_ref[bb, 0, slot] = val[bb * rows_per_state:(bb + 1) * rows_per_state]

    def rope(y):
        if is_ctx:
            return y
        return _rope(y, cos_ref[...], sin_ref[...])

    def epilogue(n, res):
        for hh in range(heads_per_tile):
            slot = n * heads_per_tile + hh
            v = res[:, hh * HEAD_DIM:(hh + 1) * HEAD_DIM]
            if slot < 8:
                out_ref[slot] = (v * Q_SCALE).astype(BF16)
            elif slot < 24:
                out_ref[slot] = v.astype(BF16)
                if is_ctx:
                    store_state(sak_ref if slot < 16 else sav_ref, slot % 8, v)
            elif slot < 32:
                out_ref[slot] = (rope(_rms(v) * gq) * Q_SCALE).astype(BF16)
            elif slot < 34:
                y = _rms(v) * gk
                if is_ctx:
                    store_state(sbk_ref, slot - 32, y)
                out_ref[slot] = rope(y).astype(BF16)
            else:
                out_ref[slot] = v.astype(BF16)
                if is_ctx:
                    store_state(sbv_ref, slot - 34, v)

    order = [6, 7, 8, 0, 1, 2, 3, 4, 5]
    res = project(order[0])
    for pos, n in enumerate(order):
        nxt = project(order[pos + 1]) if pos + 1 < len(order) else None
        epilogue(n, res)
        res = nxt


def _qkv(x2d, mods, g_pre, w_in_bf, g_q, g_k, *, is_ctx, seq, rope_tabs=None, casts=()):
    t = x2d.shape[0]
    tm = 512
    nb = max(tm // seq, 1)
    tiles_per_seq = max(seq // tm, 1)
    nmod = mods.shape[0]
    mod_idx = (lambda i: (0, 0, 0)) if nmod == 1 else (lambda i: (i // tiles_per_seq, 0, 0))
    in_specs = [
        pl.BlockSpec((tm, D_MODEL), lambda i: (i, 0)),
        pl.BlockSpec((1, 6, D_MODEL), mod_idx),
        pl.BlockSpec((1, D_MODEL), lambda i: (0, 0)),
        pl.BlockSpec((D_MODEL, IN_COLS), lambda i: (0, 0), pipeline_mode=pl.Buffered(1)),
        pl.BlockSpec((1, HEAD_DIM), lambda i: (0, 0)),
        pl.BlockSpec((1, HEAD_DIM), lambda i: (0, 0)),
    ]
    args = [x2d, mods, g_pre, w_in_bf, g_q, g_k]
    out_specs = [pl.BlockSpec((N_SLOTS, tm, HEAD_DIM), lambda i: (0, i, 0))]
    out_shape = [jax.ShapeDtypeStruct((N_SLOTS, t, HEAD_DIM), BF16)]
    if is_ctx:
        assert tm % seq == 0
        nbatch = t // seq
        for nh in (N_HEADS_A, N_HEADS_A, N_KV_B, N_KV_B):
            out_specs.append(pl.BlockSpec((nb, 1, nh, seq, HEAD_DIM), lambda i: (i, 0, 0, 0, 0)))
            out_shape.append(jax.ShapeDtypeStruct((nbatch, 1, nh, seq, HEAD_DIM), F32))
    else:
        rope_idx = lambda i: (i % tiles_per_seq, 0)
        in_specs += [pl.BlockSpec((tm, HEAD_DIM), rope_idx), pl.BlockSpec((tm, HEAD_DIM), rope_idx)]
        args += list(rope_tabs)
    for w, col_block in casts:
        w_in_spec, w_out_spec, w_shape = _cast_specs(w, t // tm, col_block=col_block)
        in_specs.append(w_in_spec)
        args.append(w)
        out_specs.append(w_out_spec)
        out_shape.append(w_shape)
    return pl.pallas_call(
        functools.partial(_qkv_kernel, is_ctx=is_ctx, tm=tm, seq=seq, n_cast=len(casts)),
        grid=(t // tm,),
        in_specs=in_specs,
        out_specs=out_specs,
        out_shape=out_shape,
        compiler_params=_cparams(("arbitrary",)),
        name="qkv_ctx" if is_ctx else "qkv_lat",
    )(*args)


def _qk(q, k):
    return lax.dot_general(q, k, (((1,), (1,)), ((), ())), preferred_element_type=F32)


def _with_ones(v):
    return jnp.concatenate([v, jnp.ones_like(v)], axis=1)


def _softmax_pv(scores, values):
    m = functools.reduce(jnp.maximum, [jnp.max(s, axis=-1, keepdims=True) for s in scores])
    with_ones = values[0].shape[1] == 2 * HEAD_DIM
    acc = None
    den = None
    for s, v in zip(scores, values):
        e = jnp.exp(s - m)
        o = jnp.dot(e.astype(BF16), v, preferred_element_type=F32)
        acc = o if acc is None else acc + o
        if not with_ones:
            l = jnp.sum(e, axis=-1, keepdims=True)
            den = l if den is None else den + l
    if with_ones:
        return acc[:, :HEAD_DIM] / acc[:, HEAD_DIM:HEAD_DIM + 1]
    return acc / den


def _attn_ctx_kernel(qkv_ref, o_ref, *, nb, seq):
    for bb in range(nb):
        rows = slice(bb * seq, (bb + 1) * seq)
        for h in range(N_HEADS_A):
            q = qkv_ref[h, rows, :]
            k = qkv_ref[N_HEADS_A + h, rows, :]
            v = qkv_ref[2 * N_HEADS_A + h, rows, :]
            o = _softmax_pv([_qk(q, k)], [v])
            o_ref[rows, h * HEAD_DIM:(h + 1) * HEAD_DIM] = o.astype(BF16)
        group = N_HEADS_B // N_KV_B
        for j in range(N_KV_B):
            k = qkv_ref[32 + j, rows, :]
            v = qkv_ref[34 + j, rows, :]
            q = jnp.concatenate([qkv_ref[24 + j * group + g, rows, :] for g in range(group)], axis=0)
            o = _softmax_pv([_qk(q, k)], [v])
            for g in range(group):
                c0 = (N_HEADS_A + j * group + g) * HEAD_DIM
                o_ref[rows, c0:c0 + HEAD_DIM] = o[g * seq:(g + 1) * seq].astype(BF16)


def _attn_ctx(qkvh, *, seq):
    t = qkvh.shape[1]
    nb = 2
    rows = nb * seq
    return pl.pallas_call(
        functools.partial(_attn_ctx_kernel, nb=nb, seq=seq),
        grid=(t // rows,),
        in_specs=[pl.BlockSpec((N_SLOTS, rows, HEAD_DIM), lambda i: (0, i, 0))],
        out_specs=pl.BlockSpec((rows, D_MODEL), lambda i: (i, 0)),
        out_shape=jax.ShapeDtypeStruct((t, D_MODEL), BF16),
        compiler_params=_cparams(("arbitrary",)),
        name="attn_ctx",
    )(qkvh)


def _attn_lat_a_kernel(q_ref, k0_ref, k1_ref, k2_ref, v0_ref, v1_ref, v2_ref,
                       ck_ref, cv_ref, bias_ref, wf_ref, o_ref, wb_ref):
    _cast_slabs([wf_ref], [wb_ref])
    for h in range(N_HEADS_A):
        ck = ck_ref[0, 0, h].astype(BF16)
        v_nb = _with_ones(jnp.concatenate([v0_ref[h], v1_ref[h], v2_ref[h]], axis=0))
        cv = _with_ones(cv_ref[0, 0, h].astype(BF16))
        for r in range(0, HALF_Q, ATT_ROWS):
            rows = slice(r, r + ATT_ROWS)
            q = q_ref[h, rows, :]
            s_nb = jnp.concatenate([_qk(q, k0_ref[h]), _qk(q, k1_ref[h]), _qk(q, k2_ref[h])], axis=1)
            s_nb = s_nb + bias_ref[h, 0, rows, :]
            o = _softmax_pv([s_nb, _qk(q, ck)], [v_nb, cv])
            o_ref[rows, h * HEAD_DIM:(h + 1) * HEAD_DIM] = o.astype(BF16)


def _attn_lat_a(qkvh, cache_k, cache_v, bias, cast_w, cast_cols):
    t = qkvh.shape[1]
    nbatch = t // 1024
    kq = 256
    w_in_spec, w_out_spec, w_shape = _cast_specs(cast_w, 2 * nbatch, col_block=cast_cols,
                                                  grid_index=lambda hf, b: hf * nbatch + b)

    def kspec(slot_blk, j):
        return pl.BlockSpec((8, kq, HEAD_DIM), lambda hf, b: (slot_blk, b * 4 + hf + j, 0))

    cache_spec = pl.BlockSpec((1, 1, N_HEADS_A, 256, HEAD_DIM), lambda hf, b: (b, 0, 0, 0, 0))
    return pl.pallas_call(
        _attn_lat_a_kernel,
        grid=(2, nbatch),
        in_specs=[
            pl.BlockSpec((8, HALF_Q, HEAD_DIM), lambda hf, b: (0, b * 2 + hf, 0)),
            kspec(1, 0), kspec(1, 1), kspec(1, 2),
            kspec(2, 0), kspec(2, 1), kspec(2, 2),
            cache_spec, cache_spec,
            pl.BlockSpec((N_HEADS_A, 1, HALF_Q, BAND_K), lambda hf, b: (0, hf, 0, 0)),
            w_in_spec,
        ],
        out_specs=[pl.BlockSpec((HALF_Q, N_HEADS_A * HEAD_DIM), lambda hf, b: (b * 2 + hf, 0)),
                   w_out_spec],
        out_shape=[jax.ShapeDtypeStruct((t, N_HEADS_A * HEAD_DIM), BF16), w_shape],
        compiler_params=_cparams(("arbitrary", "arbitrary")),
        name="attn_lat_a",
    )(qkvh, qkvh, qkvh, qkvh, qkvh, qkvh, qkvh, cache_k, cache_v, bias, cast_w)


def _attn_lat_b_kernel(q_ref, k_ref, v_ref, ck_ref, cv_ref, wf_ref, o_ref, wb_ref):
    _cast_slabs([wf_ref], [wb_ref])
    k = k_ref[0]
    v = _with_ones(v_ref[0])
    ck = ck_ref[0, 0, 0].astype(BF16)
    cv = _with_ones(cv_ref[0, 0, 0].astype(BF16))
    for g in range(N_HEADS_B // N_KV_B):
        for r in range(0, q_ref.shape[1], ATT_ROWS):
            rows = slice(r, r + ATT_ROWS)
            q = q_ref[g, rows, :]
            o = _softmax_pv([_qk(q, k), _qk(q, ck)], [v, cv])
            o_ref[rows, g * HEAD_DIM:(g + 1) * HEAD_DIM] = o.astype(BF16)


def _attn_lat_b(qkvh, cache_k, cache_v, cast_w):
    t = qkvh.shape[1]
    nbatch = t // 1024
    cache_spec = pl.BlockSpec((1, 1, 1, 256, HEAD_DIM), lambda b, j: (b, 0, j, 0, 0))
    w_in_spec, w_out_spec, w_shape = _cast_specs(cast_w, nbatch * N_KV_B,
                                                  grid_index=lambda b, j: b * N_KV_B + j)
    return pl.pallas_call(
        _attn_lat_b_kernel,
        grid=(nbatch, N_KV_B),
        in_specs=[
            pl.BlockSpec((4, 1024, HEAD_DIM), lambda b, j: (6 + j, b, 0)),
            pl.BlockSpec((1, 1024, HEAD_DIM), lambda b, j: (32 + j, b, 0)),
            pl.BlockSpec((1, 1024, HEAD_DIM), lambda b, j: (34 + j, b, 0)),
            cache_spec, cache_spec, w_in_spec,
        ],
        out_specs=[pl.BlockSpec((1024, 512), lambda b, j: (b, j)), w_out_spec],
        out_shape=[jax.ShapeDtypeStruct((t, N_HEADS_B * HEAD_DIM), BF16), w_shape],
        compiler_params=_cparams(("arbitrary", "arbitrary")),
        name="attn_lat_b",
    )(qkvh, qkvh, qkvh, cache_k, cache_v, cast_w)


def _proj_kernel(oa_ref, ob_ref, x_ref, mod_ref, w_ref, gpost_ref, gffn_ref, x1_ref, h2_ref):
    gain1 = mod_ref[0, 2:3, :] * gpost_ref[...]
    sh2 = mod_ref[0, 3:4, :]
    gain2 = gffn_ref[...] * (1.0 + mod_ref[0, 4:5, :])
    tm, ka = oa_ref.shape
    chunks = [slice(r, r + PROJ_ROWS) for r in range(0, tm, PROJ_ROWS)]

    def project(rows):
        return (jnp.dot(oa_ref[rows, :], w_ref[:ka, :], preferred_element_type=F32)
                + jnp.dot(ob_ref[rows, :], w_ref[ka:, :], preferred_element_type=F32))

    t = project(chunks[0])
    for r, rows in enumerate(chunks):
        nxt = project(chunks[r + 1]) if r + 1 < len(chunks) else None
        x1 = x_ref[rows, :] + gain1 * _rms(t)
        x1_ref[rows, :] = x1
        h2_ref[rows, :] = (_rms(x1) * gain2 + sh2).astype(BF16)
        t = nxt


def _proj(o_a, o_b, x2d, mods, w_out_bf, g_post, g_ffn, *, tiles_per_mod):
    t = x2d.shape[0]
    tm = 512
    nmod = mods.shape[0]
    mod_idx = (lambda i: (0, 0, 0)) if nmod == 1 else (lambda i: (i // tiles_per_mod, 0, 0))
    ka = D_MODEL // 2
    ob_idx = (lambda i: (i, 1)) if o_b.shape[1] == D_MODEL else (lambda i: (i, 0))
    in_specs = [
        pl.BlockSpec((tm, ka), lambda i: (i, 0)),
        pl.BlockSpec((tm, ka), ob_idx),
        pl.BlockSpec((tm, D_MODEL), lambda i: (i, 0)),
        pl.BlockSpec((1, 6, D_MODEL), mod_idx),
        pl.BlockSpec((D_MODEL, D_MODEL), lambda i: (0, 0), pipeline_mode=pl.Buffered(1)),
        pl.BlockSpec((1, D_MODEL), lambda i: (0, 0)),
        pl.BlockSpec((1, D_MODEL), lambda i: (0, 0)),
    ]
    out_specs = [
        pl.BlockSpec((tm, D_MODEL), lambda i: (i, 0)),
        pl.BlockSpec((tm, D_MODEL), lambda i: (i, 0)),
    ]
    out_shape = [
        jax.ShapeDtypeStruct((t, D_MODEL), F32),
        jax.ShapeDtypeStruct((t, D_MODEL), BF16),
    ]
    return pl.pallas_call(
        _proj_kernel,
        grid=(t // tm,),
        in_specs=in_specs,
        out_specs=out_specs,
        out_shape=out_shape,
        compiler_params=_cparams(("arbitrary",)),
        name="proj",
    )(o_a, o_b, x2d, mods, w_out_bf, g_post, g_ffn)


def _ffn_kernel(h2_ref, x1_ref, mod_ref, wv_ref, wg_ref, cpv_ref, cpg_ref,
                wd_ref, gpost_ref, out_ref, x1_buf, x1_sem, *u_scr, tm, seq):
    i = pl.program_id(0)
    j = pl.program_id(1)
    nseq = tm // seq
    stride = seq + 8
    chunks = [slice(c * FFN_CW, (c + 1) * FFN_CW) for c in range(FFN_TF // FFN_CW)]

    def x1_copy():
        return pltpu.make_async_copy(x1_ref.at[pl.ds(pl.multiple_of(i * tm, tm), tm), :], x1_buf, x1_sem)

    def conv(scr, base, cp_ref):
        lo = scr[base - 1:base - 1 + seq, :]
        mid = scr[base:base + seq, :]
        hi = scr[base + 1:base + 1 + seq, :]
        return (mid * cp_ref[1:2, :] + cp_ref[3:4, :]) + lo * cp_ref[0:1, :] + hi * cp_ref[2:3, :]

    def step(first):
        h2 = h2_ref[...]
        for c, cols in enumerate(chunks):
            for k, w_ref in enumerate((wv_ref, wg_ref)):
                u = jnp.dot(h2, w_ref[:, cols], preferred_element_type=F32)
                for s in range(nseq):
                    u_scr[2 * c + k][8 + s * stride:8 + s * stride + seq, :] = u[s * seq:(s + 1) * seq]
        for c, cols in enumerate(chunks):
            acts = []
            for s in range(nseq):
                base = 8 + s * stride
                val = conv(u_scr[2 * c], base, cpv_ref.at[:, cols])
                gate = conv(u_scr[2 * c + 1], base, cpg_ref.at[:, cols])
                acts.append((gate / (1.0 + jnp.exp(-gate)) * val).astype(BF16))
            act = acts[0] if nseq == 1 else jnp.concatenate(acts, axis=0)
            d = jnp.dot(act, wd_ref[cols, :], preferred_element_type=F32)
            if first and c == 0:
                out_ref[...] = d
            else:
                out_ref[...] += d

    @pl.when(j == 0)
    def _():
        x1_copy().start()
        for scr in u_scr:
            for s in range(nseq + 1):
                scr[s * stride:s * stride + 8, :] = jnp.zeros((8, FFN_CW), F32)
        step(True)

    @pl.when(j > 0)
    def _():
        step(False)

    @pl.when(j == N_FFN_TILES - 1)
    def _():
        x1_copy().wait()
        gain = mod_ref[0, 5:6, :] * gpost_ref[...]
        for r in range(0, tm, FIN_ROWS):
            rows = slice(r, r + FIN_ROWS)
            out_ref[rows, :] = x1_buf[rows, :] + gain * _rms(out_ref[rows, :])


def _ffn(h2, x1, mods, w_val_bf, w_gate_bf, conv_p, w_down_bf, g_post, *, seq, tm):
    t = x1.shape[0]
    nmod = mods.shape[0]
    mod_idx = (lambda i, j: (0, 0, 0)) if nmod == 1 else (lambda i, j: (i * tm // seq, 0, 0))
    nj = N_FFN_TILES
    u_rows = 8 + (tm // seq) * (seq + 8)
    return pl.pallas_call(
        functools.partial(_ffn_kernel, tm=tm, seq=seq),
        grid=(t // tm, nj),
        in_specs=[
            pl.BlockSpec((tm, D_MODEL), lambda i, j: (i, 0)),
            pl.BlockSpec(memory_space=pl.ANY),
            pl.BlockSpec((1, 6, D_MODEL), mod_idx),
            pl.BlockSpec((D_MODEL, FFN_TF), lambda i, j: (0, j)),
            pl.BlockSpec((D_MODEL, FFN_TF), lambda i, j: (0, j)),
            pl.BlockSpec((4, FFN_TF), lambda i, j: (0, j)),
            pl.BlockSpec((4, FFN_TF), lambda i, j: (0, nj + j)),
            pl.BlockSpec((FFN_TF, D_MODEL), lambda i, j: (j, 0)),
            pl.BlockSpec((1, D_MODEL), lambda i, j: (0, 0)),
        ],
        out_specs=pl.BlockSpec((tm, D_MODEL), lambda i, j: (i, 0)),
        out_shape=jax.ShapeDtypeStruct((t, D_MODEL), F32),
        scratch_shapes=[pltpu.VMEM((tm, D_MODEL), F32), pltpu.SemaphoreType.DMA(())]
        + [pltpu.VMEM((u_rows, FFN_CW), F32) for _ in range(2 * (FFN_TF // FFN_CW))],
        compiler_params=_cparams(("arbitrary", "arbitrary")),
        name="ffn",
    )(h2, x1, mods, w_val_bf, w_gate_bf, conv_p, conv_p, w_down_bf, g_post)


def _rope_tables(n_tok):
    half = HEAD_DIM // 2
    t = np.arange(n_tok)
    freqs = ROPE_THETA ** (-np.arange(0, half, 2, dtype=np.float64) / half)
    lane = np.arange(HEAD_DIM)
    pos = np.where(lane[None, :] < half, (t // GRID_W)[:, None], (t % GRID_W)[:, None])
    ang = pos * freqs[lane % (half // 2)][None, :]
    sign = np.where((lane % half) < half // 2, -1.0, 1.0)[None, :]
    return (jnp.asarray(np.cos(ang), F32), jnp.asarray(np.sin(ang) * sign, F32))


def kernel(x_prompt, x_sample, c, cache_a_k, cache_a_v, cache_b_k, cache_b_v, c_ctx, w_mod, b_mod,
           g_attn_pre, g_attn_post, g_ffn_pre, g_ffn_post, w_in, rpb, g_qnorm, g_knorm, w_out,
           w_up, conv_w, conv_b, w_down):
    batch, seq, _ = x_prompt.shape
    dec_batch, dec_seq, _ = x_sample.shape
    depth = w_mod.shape[0]
    assert depth == 1 and dec_seq == GRID_H * GRID_W

    xp = x_prompt.reshape(batch * seq, D_MODEL)
    xs = x_sample.reshape(dec_batch * dec_seq, D_MODEL)
    l = 0
    conds = jnp.concatenate(
        [c_ctx[None, :], c, jnp.zeros((16 - 1 - dec_batch, D_MODEL), F32)], axis=0)
    mods = _modulation(conds, w_mod[l], b_mod[l]).reshape(16, 6, D_MODEL)
    mod_ctx = mods[0:1]
    mod_lat = mods[1:1 + dec_batch]

    g_pre = g_attn_pre[l].reshape(1, D_MODEL)
    g_post = g_attn_post[l].reshape(1, D_MODEL)
    g_fpre = g_ffn_pre[l].reshape(1, D_MODEL)
    g_fpost = g_ffn_post[l].reshape(1, D_MODEL)
    g_q = g_qnorm[l].reshape(1, HEAD_DIM)
    g_k = g_knorm[l].reshape(1, HEAD_DIM)
    conv_p = jnp.concatenate([conv_w[l], conv_b[l].reshape(1, 2 * D_FF)], axis=0)

    bias, w_in_bf = _bias_table(rpb[l], w_in[l])

    qkvh_c, st_ak, st_av, st_bk, st_bv = _qkv(
        xp, mod_ctx, g_pre, w_in_bf, g_q, g_k, is_ctx=True, seq=seq)
    o_c = _attn_ctx(qkvh_c, seq=seq)
    qkvh_s, w_gate_bf, w_out_bf = _qkv(
        xs, mod_lat, g_pre, w_in_bf, g_q, g_k, is_ctx=False, seq=dec_seq,
        rope_tabs=_rope_tables(dec_seq), casts=[(w_up[l], (1, 2)), (w_out[l], (0, 1))])
    x1_c, h2_c = _proj(o_c, o_c, xp, mod_ctx, w_out_bf, g_post, g_fpre, tiles_per_mod=1)
    o_sa, w_val_bf = _attn_lat_a(qkvh_s, cache_a_k, cache_a_v, bias, w_up[l], (0, 2))
    o_sb, w_down_bf = _attn_lat_b(qkvh_s, cache_b_k, cache_b_v, w_down[l])
    x1_s, h2_s = _proj(o_sa, o_sb, xs, mod_lat, w_out_bf, g_post, g_fpre,
                       tiles_per_mod=dec_seq // 512)

    y_c = _ffn(h2_c, x1_c, mod_ctx, w_val_bf, w_gate_bf, conv_p, w_down_bf, g_fpost, seq=seq, tm=1024)
    y_s = _ffn(h2_s, x1_s, mod_lat, w_val_bf, w_gate_bf, conv_p, w_down_bf, g_fpost,
               seq=dec_seq, tm=1024)

    return (y_c.reshape(batch, seq, D_MODEL), y_s.reshape(dec_batch, dec_seq, D_MODEL),
            st_ak, st_av, st_bk, st_bv)
```

```python
import functools

import numpy as np
import jax
import jax.numpy as jnp
from jax import lax
from jax.experimental import pallas as pl
from jax.experimental.pallas import tpu as pltpu

F32 = jnp.float32
BF16 = jnp.bfloat16

D_MODEL = 2048
HEAD_DIM = 128
N_HEADS_A = 8
N_HEADS_B = 8
N_KV_B = 2
GRID_W = 64
GRID_H = 16
WIN_H = 8
WIN_W = 16
D_FF = 5632
ROPE_THETA = 10000.0
EPS = 1e-6
NEG_INF = -1e30
Q_SCALE = HEAD_DIM ** -0.5
IN_COLS = 4608
N_SLOTS = IN_COLS // HEAD_DIM
QKV_TN = 512
N_QKV_TILES = IN_COLS // QKV_TN
FFN_TF = 512
FFN_CW = 256
N_FFN_TILES = D_FF // FFN_TF
FIN_ROWS = 128
PROJ_ROWS = 128
ATT_ROWS = 128
HALF_Q = 512
BAND_K = 768
VMEM_LIMIT = 60 * 1024 * 1024


def _cparams(sem):
    return pltpu.CompilerParams(dimension_semantics=sem, vmem_limit_bytes=VMEM_LIMIT)


def _rms(x):
    return x * lax.rsqrt(jnp.mean(x * x, axis=-1, keepdims=True) + EPS)


def _mod_kernel(c_ref, w_ref, b_ref, o_ref):
    c = c_ref[...]
    a = (c / (1.0 + jnp.exp(-c))).astype(BF16)
    o_ref[...] = jnp.dot(a, w_ref[...].astype(BF16), preferred_element_type=F32) + b_ref[...]


def _modulation(conds, w_mod, b_mod):
    tn = 2048
    n = w_mod.shape[1]
    return pl.pallas_call(
        _mod_kernel,
        grid=(n // tn,),
        in_specs=[
            pl.BlockSpec((16, D_MODEL), lambda j: (0, 0)),
            pl.BlockSpec((D_MODEL, tn), lambda j: (0, j)),
            pl.BlockSpec((1, tn), lambda j: (0, j)),
        ],
        out_specs=pl.BlockSpec((16, tn), lambda j: (0, j)),
        out_shape=jax.ShapeDtypeStruct((16, n), F32),
        compiler_params=_cparams(("arbitrary",)),
        name="modulation",
    )(conds, w_mod, b_mod.reshape(1, n))


def _bias_kernel(rpb_ref, wf_ref, o_ref, wb_ref):
    _cast_slabs([wf_ref], [wb_ref])
    h = pl.program_id(0)
    qc = lax.broadcasted_iota(jnp.int32, (GRID_W, 128), 0)
    lane = lax.broadcasted_iota(jnp.int32, (GRID_W, 128), 1)
    kc = lane & (GRID_W - 1)
    dmat = kc - qc + (WIN_W - 1)
    cs = jnp.clip(qc - WIN_W // 2, 0, GRID_W - WIN_W)
    col_valid = (kc >= cs) & (kc < cs + WIN_W)
    neg = jnp.full((GRID_W, 128), NEG_INF, F32)

    g = []
    for dr in range(2 * WIN_H - 1):
        acc = neg
        for d in range(2 * WIN_W - 1):
            acc = jnp.where(dmat == d, rpb_ref[h, dr * (2 * WIN_W - 1) + d], acc)
        g.append(jnp.where(col_valid, acc, neg))

    def row_tile(r, kr):
        rs = min(max(r - WIN_H // 2, 0), GRID_H - WIN_H)
        if rs <= kr < rs + WIN_H:
            return g[kr - r + WIN_H - 1]
        return neg

    for half in range(2):
        for lr in range(HALF_Q // GRID_W):
            r = 8 * half + lr
            for p in range(BAND_K // 128):
                kr0 = 4 * half + 2 * p
                tile = jnp.where(lane < GRID_W, row_tile(r, kr0), row_tile(r, kr0 + 1))
                o_ref[0, half, lr * GRID_W:(lr + 1) * GRID_W, p * 128:(p + 1) * 128] = tile


def _bias_table(rpb_l, cast_w):
    rpb2 = rpb_l.reshape(N_HEADS_A, (2 * WIN_H - 1) * (2 * WIN_W - 1))
    w_in_spec, w_out_spec, w_shape = _cast_specs(cast_w, N_HEADS_A)
    return pl.pallas_call(
        _bias_kernel,
        grid=(N_HEADS_A,),
        in_specs=[pl.BlockSpec(memory_space=pltpu.SMEM), w_in_spec],
        out_specs=[pl.BlockSpec((1, 2, HALF_Q, BAND_K), lambda h: (h, 0, 0, 0)), w_out_spec],
        out_shape=[jax.ShapeDtypeStruct((N_HEADS_A, 2, HALF_Q, BAND_K), F32), w_shape],
        compiler_params=_cparams(("arbitrary",)),
        name="bias_table",
    )(rpb2, cast_w)


def _cast_specs(w, steps, *, col_block=(0, 1), grid_index=lambda i: i):
    k, n = col_block
    rows = w.shape[0] // steps
    cols = w.shape[1] // n
    assert rows * steps == w.shape[0] and rows % 16 == 0 and cols * n == w.shape[1]
    in_spec = pl.BlockSpec((rows, cols), lambda *g: (grid_index(*g), k))
    out_spec = pl.BlockSpec((rows, cols), lambda *g: (grid_index(*g), 0))
    return in_spec, out_spec, jax.ShapeDtypeStruct((w.shape[0], cols), BF16)


def _cast_slabs(cast_in, cast_out):
    for wf_ref, wb_ref in zip(cast_in, cast_out):
        wb_ref[...] = wf_ref[...].astype(BF16)


def _rope(y, cos, sin_signed):
    lane = lax.broadcasted_iota(jnp.int32, y.shape, 1)
    swapped = jnp.where((lane & 63) < 32, pltpu.roll(y, 96, 1), pltpu.roll(y, 32, 1))
    return y * cos + swapped * sin_signed


def _qkv_kernel(*refs, is_ctx, tm, seq, n_cast):
    n_in = (6 if is_ctx else 8) + n_cast
    _cast_slabs(refs[n_in - n_cast:n_in], refs[len(refs) - n_cast:])
    if is_ctx:
        x_ref, mod_ref, gpre_ref, w_ref, gq_ref, gk_ref = refs[:6]
        out_ref, sak_ref, sav_ref, sbk_ref, sbv_ref = refs[n_in:n_in + 5]
    else:
        x_ref, mod_ref, gpre_ref, w_ref, gq_ref, gk_ref, cos_ref, sin_ref = refs[:8]
        out_ref = refs[n_in]
    nb = max(tm // seq, 1)
    rows_per_state = min(seq, tm)
    heads_per_tile = QKV_TN // HEAD_DIM
    sh1 = mod_ref[0, 0:1, :]
    sc1 = mod_ref[0, 1:2, :]
    h = (_rms(x_ref[...]) * (gpre_ref[...] * (1.0 + sc1)) + sh1).astype(BF16)
    gq = gq_ref[...]
    gk = gk_ref[...]

    def project(n):
        return jnp.dot(h, w_ref[:, n * QKV_TN:(n + 1) * QKV_TN], preferred_element_type=F32)

    def store_state(s_ref, slot, val):
        for bb in range(nb):
            s_ref[bb, 0, slot] = val[bb * rows_per_state:(bb + 1) * rows_per_state]

    def rope(y):
        if is_ctx:
            return y
        return _rope(y, cos_ref[...], sin_ref[...])

    def epilogue(n, res):
        for hh in range(heads_per_tile):
            slot = n * heads_per_tile + hh
            v = res[:, hh * HEAD_DIM:(hh + 1) * HEAD_DIM]
            if slot < 8:
                out_ref[slot] = (v * Q_SCALE).astype(BF16)
            elif slot < 24:
                out_ref[slot] = v.astype(BF16)
                if is_ctx:
                    store_state(sak_ref if slot < 16 else sav_ref, slot % 8, v)
            elif slot < 32:
                out_ref[slot] = (rope(_rms(v) * gq) * Q_SCALE).astype(BF16)
            elif slot < 34:
                y = _rms(v) * gk
                if is_ctx:
                    store_state(sbk_ref, slot - 32, y)
                out_ref[slot] = rope(y).astype(BF16)
            else:
                out_ref[slot] = v.astype(BF16)
                if is_ctx:
                    store_state(sbv_ref, slot - 34, v)

    order = [6, 7, 8, 0, 1, 2, 3, 4, 5]
    res = project(order[0])
    for pos, n in enumerate(order):
        nxt = project(order[pos + 1]) if pos + 1 < len(order) else None
        epilogue(n, res)
        res = nxt


def _qkv(x2d, mods, g_pre, w_in_bf, g_q, g_k, *, is_ctx, seq, rope_tabs=None, casts=()):
    t = x2d.shape[0]
    tm = 512
    nb = max(tm // seq, 1)
    tiles_per_seq = max(seq // tm, 1)
    nmod = mods.shape[0]
    mod_idx = (lambda i: (0, 0, 0)) if nmod == 1 else (lambda i: (i // tiles_per_seq, 0, 0))
    in_specs = [
        pl.BlockSpec((tm, D_MODEL), lambda i: (i, 0)),
        pl.BlockSpec((1, 6, D_MODEL), mod_idx),
        pl.BlockSpec((1, D_MODEL), lambda i: (0, 0)),
        pl.BlockSpec((D_MODEL, IN_COLS), lambda i: (0, 0), pipeline_mode=pl.Buffered(1)),
        pl.BlockSpec((1, HEAD_DIM), lambda i: (0, 0)),
        pl.BlockSpec((1, HEAD_DIM), lambda i: (0, 0)),
    ]
    args = [x2d, mods, g_pre, w_in_bf, g_q, g_k]
    out_specs = [pl.BlockSpec((N_SLOTS, tm, HEAD_DIM), lambda i: (0, i, 0))]
    out_shape = [jax.ShapeDtypeStruct((N_SLOTS, t, HEAD_DIM), BF16)]
    if is_ctx:
        assert tm % seq == 0
        nbatch = t // seq
        for nh in (N_HEADS_A, N_HEADS_A, N_KV_B, N_KV_B):
            out_specs.append(pl.BlockSpec((nb, 1, nh, seq, HEAD_DIM), lambda i: (i, 0, 0, 0, 0)))
            out_shape.append(jax.ShapeDtypeStruct((nbatch, 1, nh, seq, HEAD_DIM), F32))
    else:
        rope_idx = lambda i: (i % tiles_per_seq, 0)
        in_specs += [pl.BlockSpec((tm, HEAD_DIM), rope_idx), pl.BlockSpec((tm, HEAD_DIM), rope_idx)]
        args += list(rope_tabs)
    for w, col_block in casts:
        w_in_spec, w_out_spec, w_shape = _cast_specs(w, t // tm, col_block=col_block)
        in_specs.append(w_in_spec)
        args.append(w)
        out_specs.append(w_out_spec)
        out_shape.append(w_shape)
    return pl.pallas_call(
        functools.partial(_qkv_kernel, is_ctx=is_ctx, tm=tm, seq=seq, n_cast=len(casts)),
        grid=(t // tm,),
        in_specs=in_specs,
        out_specs=out_specs,
        out_shape=out_shape,
        compiler_params=_cparams(("arbitrary",)),
        name="qkv_ctx" if is_ctx else "qkv_lat",
    )(*args)


def _qk(q, k):
    return lax.dot_general(q, k, (((1,), (1,)), ((), ())), preferred_element_type=F32)


def _with_ones(v):
    return jnp.concatenate([v, jnp.ones_like(v)], axis=1)


def _softmax_pv(scores, values):
    m = functools.reduce(jnp.maximum, [jnp.max(s, axis=-1, keepdims=True) for s in scores])
    with_ones = values[0].shape[1] == 2 * HEAD_DIM
    acc = None
    den = None
    for s, v in zip(scores, values):
        e = jnp.exp(s - m)
        o = jnp.dot(e.astype(BF16), v, preferred_element_type=F32)
        acc = o if acc is None else acc + o
        if not with_ones:
            l = jnp.sum(e, axis=-1, keepdims=True)
            den = l if den is None else den + l
    if with_ones:
        return acc[:, :HEAD_DIM] / acc[:, HEAD_DIM:HEAD_DIM + 1]
    return acc / den


def _attn_ctx_kernel(qkv_ref, o_ref, *, nb, seq):
    for bb in range(nb):
        rows = slice(bb * seq, (bb + 1) * seq)
        for h in range(N_HEADS_A):
            q = qkv_ref[h, rows, :]
            k = qkv_ref[N_HEADS_A + h, rows, :]
            v = qkv_ref[2 * N_HEADS_A + h, rows, :]
            o = _softmax_pv([_qk(q, k)], [v])
            o_ref[rows, h * HEAD_DIM:(h + 1) * HEAD_DIM] = o.astype(BF16)
        group = N_HEADS_B // N_KV_B
        for j in range(N_KV_B):
            k = qkv_ref[32 + j, rows, :]
            v = qkv_ref[34 + j, rows, :]
            q = jnp.concatenate([qkv_ref[24 + j * group + g, rows, :] for g in range(group)], axis=0)
            o = _softmax_pv([_qk(q, k)], [v])
            for g in range(group):
                c0 = (N_HEADS_A + j * group + g) * HEAD_DIM
                o_ref[rows, c0:c0 + HEAD_DIM] = o[g * seq:(g + 1) * seq].astype(BF16)


def _attn_ctx(qkvh, *, seq):
    t = qkvh.shape[1]
    nb = 2
    rows = nb * seq
    return pl.pallas_call(
        functools.partial(_attn_ctx_kernel, nb=nb, seq=seq),
        grid=(t // rows,),
        in_specs=[pl.BlockSpec((N_SLOTS, rows, HEAD_DIM), lambda i: (0, i, 0))],
        out_specs=pl.BlockSpec((rows, D_MODEL), lambda i: (i, 0)),
        out_shape=jax.ShapeDtypeStruct((t, D_MODEL), BF16),
        compiler_params=_cparams(("arbitrary",)),
        name="attn_ctx",
    )(qkvh)


def _attn_lat_a_kernel(q_ref, k0_ref, k1_ref, k2_ref, v0_ref, v1_ref, v2_ref,
                       ck_ref, cv_ref, bias_ref, wf_ref, o_ref, wb_ref):
    _cast_slabs([wf_ref], [wb_ref])
    for h in range(N_HEADS_A):
        ck = ck_ref[0, 0, h].astype(BF16)
        v_nb = _with_ones(jnp.concatenate([v0_ref[h], v1_ref[h], v2_ref[h]], axis=0))
        cv = _with_ones(cv_ref[0, 0, h].astype(BF16))
        for r in range(0, HALF_Q, ATT_ROWS):
            rows = slice(r, r + ATT_ROWS)
            q = q_ref[h, rows, :]
            s_nb = jnp.concatenate([_qk(q, k0_ref[h]), _qk(q, k1_ref[h]), _qk(q, k2_ref[h])], axis=1)
            s_nb = s_nb + bias_ref[h, 0, rows, :]
            o = _softmax_pv([s_nb, _qk(q, ck)], [v_nb, cv])
            o_ref[rows, h * HEAD_DIM:(h + 1) * HEAD_DIM] = o.astype(BF16)


def _attn_lat_a(qkvh, cache_k, cache_v, bias, cast_w, cast_cols):
    t = qkvh.shape[1]
    nbatch = t // 1024
    kq = 256
    w_in_spec, w_out_spec, w_shape = _cast_specs(cast_w, 2 * nbatch, col_block=cast_cols,
                                                  grid_index=lambda hf, b: hf * nbatch + b)

    def kspec(slot_blk, j):
        return pl.BlockSpec((8, kq, HEAD_DIM), lambda hf, b: (slot_blk, b * 4 + hf + j, 0))

    cache_spec = pl.BlockSpec((1, 1, N_HEADS_A, 256, HEAD_DIM), lambda hf, b: (b, 0, 0, 0, 0))
    return pl.pallas_call(
        _attn_lat_a_kernel,
        grid=(2, nbatch),
        in_specs=[
            pl.BlockSpec((8, HALF_Q, HEAD_DIM), lambda hf, b: (0, b * 2 + hf, 0)),
            kspec(1, 0), kspec(1, 1), kspec(1, 2),
            kspec(2, 0), kspec(2, 1), kspec(2, 2),
            cache_spec, cache_spec,
            pl.BlockSpec((N_HEADS_A, 1, HALF_Q, BAND_K), lambda hf, b: (0, hf, 0, 0)),
            w_in_spec,
        ],
        out_specs=[pl.BlockSpec((HALF_Q, N_HEADS_A * HEAD_DIM), lambda hf, b: (b * 2 + hf, 0)),
                   w_out_spec],
        out_shape=[jax.ShapeDtypeStruct((t, N_HEADS_A * HEAD_DIM), BF16), w_shape],
        compiler_params=_cparams(("arbitrary", "arbitrary")),
        name="attn_lat_a",
    )(qkvh, qkvh, qkvh, qkvh, qkvh, qkvh, qkvh, cache_k, cache_v, bias, cast_w)


def _attn_lat_b_kernel(q_ref, k_ref, v_ref, ck_ref, cv_ref, wf_ref, o_ref, wb_ref):
    _cast_slabs([wf_ref], [wb_ref])
    k = k_ref[0]
    v = _with_ones(v_ref[0])
    ck = ck_ref[0, 0, 0].astype(BF16)
    cv = _with_ones(cv_ref[0, 0, 0].astype(BF16))
    for g in range(N_HEADS_B // N_KV_B):
        for r in range(0, q_ref.shape[1], ATT_ROWS):
            rows = slice(r, r + ATT_ROWS)
            q = q_ref[g, rows, :]
            o = _softmax_pv([_qk(q, k), _qk(q, ck)], [v, cv])
            o_ref[rows, g * HEAD_DIM:(g + 1) * HEAD_DIM] = o.astype(BF16)


def _attn_lat_b(qkvh, cache_k, cache_v, cast_w):
    t = qkvh.shape[1]
    nbatch = t // 1024
    cache_spec = pl.BlockSpec((1, 1, 1, 256, HEAD_DIM), lambda b, j: (b, 0, j, 0, 0))
    w_in_spec, w_out_spec, w_shape = _cast_specs(cast_w, nbatch * N_KV_B,
                                                  grid_index=lambda b, j: b * N_KV_B + j)
    return pl.pallas_call(
        _attn_lat_b_kernel,
        grid=(nbatch, N_KV_B),
        in_specs=[
            pl.BlockSpec((4, 1024, HEAD_DIM), lambda b, j: (6 + j, b, 0)),
            pl.BlockSpec((1, 1024, HEAD_DIM), lambda b, j: (32 + j, b, 0)),
            pl.BlockSpec((1, 1024, HEAD_DIM), lambda b, j: (34 + j, b, 0)),
            cache_spec, cache_spec, w_in_spec,
        ],
        out_specs=[pl.BlockSpec((1024, 512), lambda b, j: (b, j)), w_out_spec],
        out_shape=[jax.ShapeDtypeStruct((t, N_HEADS_B * HEAD_DIM), BF16), w_shape],
        compiler_params=_cparams(("arbitrary", "arbitrary")),
        name="attn_lat_b",
    )(qkvh, qkvh, qkvh, cache_k, cache_v, cast_w)


def _proj_kernel(oa_ref, ob_ref, x_ref, mod_ref, w_ref, gpost_ref, gffn_ref, x1_ref, h2_ref):
    gain1 = mod_ref[0, 2:3, :] * gpost_ref[...]
    sh2 = mod_ref[0, 3:4, :]
    gain2 = gffn_ref[...] * (1.0 + mod_ref[0, 4:5, :])
    tm, ka = oa_ref.shape
    chunks = [slice(r, r + PROJ_ROWS) for r in range(0, tm, PROJ_ROWS)]

    def project(rows):
        return (jnp.dot(oa_ref[rows, :], w_ref[:ka, :], preferred_element_type=F32)
                + jnp.dot(ob_ref[rows, :], w_ref[ka:, :], preferred_element_type=F32))

    t = project(chunks[0])
    for r, rows in enumerate(chunks):
        nxt = project(chunks[r + 1]) if r + 1 < len(chunks) else None
        x1 = x_ref[rows, :] + gain1 * _rms(t)
        x1_ref[rows, :] = x1
        h2 = (_rms(x1) * gain2 + sh2).astype(BF16)
        h2_ref[rows.start // 2:rows.stop // 2, :] = pltpu.bitcast(h2, jnp.uint32)
        t = nxt


def _proj(o_a, o_b, x2d, mods, w_out_bf, g_post, g_ffn, *, tiles_per_mod):
    t = x2d.shape[0]
    tm = 512
    nmod = mods.shape[0]
    mod_idx = (lambda i: (0, 0, 0)) if nmod == 1 else (lambda i: (i // tiles_per_mod, 0, 0))
    ka = D_MODEL // 2
    ob_idx = (lambda i: (i, 1)) if o_b.shape[1] == D_MODEL else (lambda i: (i, 0))
    in_specs = [
        pl.BlockSpec((tm, ka), lambda i: (i, 0)),
        pl.BlockSpec((tm, ka), ob_idx),
        pl.BlockSpec((tm, D_MODEL), lambda i: (i, 0)),
        pl.BlockSpec((1, 6, D_MODEL), mod_idx),
        pl.BlockSpec((D_MODEL, D_MODEL), lambda i: (0, 0), pipeline_mode=pl.Buffered(1)),
        pl.BlockSpec((1, D_MODEL), lambda i: (0, 0)),
        pl.BlockSpec((1, D_MODEL), lambda i: (0, 0)),
    ]
    out_specs = [
        pl.BlockSpec((tm, D_MODEL), lambda i: (i, 0)),
        pl.BlockSpec((tm // 2, D_MODEL), lambda i: (i, 0)),
    ]
    out_shape = [
        jax.ShapeDtypeStruct((t, D_MODEL), F32),
        jax.ShapeDtypeStruct((t // 2, D_MODEL), jnp.uint32),
    ]
    return pl.pallas_call(
        _proj_kernel,
        grid=(t // tm,),
        in_specs=in_specs,
        out_specs=out_specs,
        out_shape=out_shape,
        compiler_params=_cparams(("arbitrary",)),
        name="proj",
    )(o_a, o_b, x2d, mods, w_out_bf, g_post, g_ffn)


def _ffn_kernel(h2_ref, x1_ref, mod_ref, wv_ref, wg_ref, cpv_ref, cpg_ref,
                wd_ref, gpost_ref, out_ref, x1_buf, x1_sem, *u_scr, tm, seq):
    i = pl.program_id(0)
    j = pl.program_id(1)
    nseq = tm // seq
    stride = seq + 8
    chunks = [slice(c * FFN_CW, (c + 1) * FFN_CW) for c in range(FFN_TF // FFN_CW)]

    def x1_copy():
        return pltpu.make_async_copy(x1_ref.at[pl.ds(pl.multiple_of(i * tm, tm), tm), :], x1_buf, x1_sem)

    def conv(scr, base, cp_ref):
        lo = scr[base - 1:base - 1 + seq, :]
        mid = scr[base:base + seq, :]
        hi = scr[base + 1:base + 1 + seq, :]
        return (mid * cp_ref[1:2, :] + cp_ref[3:4, :]) + lo * cp_ref[0:1, :] + hi * cp_ref[2:3, :]

    def step(first):
        h2 = pltpu.bitcast(h2_ref[...], BF16)
        for c, cols in enumerate(chunks):
            for k, w_ref in enumerate((wv_ref, wg_ref)):
                u = jnp.dot(h2, w_ref[:, cols], preferred_element_type=F32)
                for s in range(nseq):
                    u_scr[2 * c + k][8 + s * stride:8 + s * stride + seq, :] = u[s * seq:(s + 1) * seq]
        for c, cols in enumerate(chunks):
            acts = []
            for s in range(nseq):
                base = 8 + s * stride
                val = conv(u_scr[2 * c], base, cpv_ref.at[:, cols])
                gate = conv(u_scr[2 * c + 1], base, cpg_ref.at[:, cols])
                acts.append((gate / (1.0 + jnp.exp(-gate)) * val).astype(BF16))
            act = acts[0] if nseq == 1 else jnp.concatenate(acts, axis=0)
            d = jnp.dot(act, wd_ref[cols, :], preferred_element_type=F32)
            if first and c == 0:
                out_ref[...] = d
            else:
                out_ref[...] += d

    @pl.when(j == 0)
    def _():
        x1_copy().start()
        for scr in u_scr:
            for s in range(nseq + 1):
                scr[s * stride:s * stride + 8, :] = jnp.zeros((8, FFN_CW), F32)
        step(True)

    @pl.when(j > 0)
    def _():
        step(False)

    @pl.when(j == N_FFN_TILES - 1)
    def _():
        x1_copy().wait()
        gain = mod_ref[0, 5:6, :] * gpost_ref[...]
        for r in range(0, tm, FIN_ROWS):
            rows = slice(r, r + FIN_ROWS)
            out_ref[rows, :] = x1_buf[rows, :] + gain * _rms(out_ref[rows, :])


def _ffn(h2, x1, mods, w_val_bf, w_gate_bf, conv_p, w_down_bf, g_post, *, seq, tm):
    t = x1.shape[0]
    nmod = mods.shape[0]
    mod_idx = (lambda i, j: (0, 0, 0)) if nmod == 1 else (lambda i, j: (i * tm // seq, 0, 0))
    nj = N_FFN_TILES
    u_rows = 8 + (tm // seq) * (seq + 8)
    return pl.pallas_call(
        functools.partial(_ffn_kernel, tm=tm, seq=seq),
        grid=(t // tm, nj),
        in_specs=[
            pl.BlockSpec((tm // 2, D_MODEL), lambda i, j: (i, 0)),
            pl.BlockSpec(memory_space=pl.ANY),
            pl.BlockSpec((1, 6, D_MODEL), mod_idx),
            pl.BlockSpec((D_MODEL, FFN_TF), lambda i, j: (0, j)),
            pl.BlockSpec((D_MODEL, FFN_TF), lambda i, j: (0, j)),
            pl.BlockSpec((4, FFN_TF), lambda i, j: (0, j)),
            pl.BlockSpec((4, FFN_TF), lambda i, j: (0, nj + j)),
            pl.BlockSpec((FFN_TF, D_MODEL), lambda i, j: (j, 0)),
            pl.BlockSpec((1, D_MODEL), lambda i, j: (0, 0)),
        ],
        out_specs=pl.BlockSpec((tm, D_MODEL), lambda i, j: (i, 0)),
        out_shape=jax.ShapeDtypeStruct((t, D_MODEL), F32),
        scratch_shapes=[pltpu.VMEM((tm, D_MODEL), F32), pltpu.SemaphoreType.DMA(())]
        + [pltpu.VMEM((u_rows, FFN_CW), F32) for _ in range(2 * (FFN_TF // FFN_CW))],
        compiler_params=_cparams(("arbitrary", "arbitrary")),
        name="ffn",
    )(h2, x1, mods, w_val_bf, w_gate_bf, conv_p, conv_p, w_down_bf, g_post)


def _rope_tables(n_tok):
    half = HEAD_DIM // 2
    t = np.arange(n_tok)
    freqs = ROPE_THETA ** (-np.arange(0, half, 2, dtype=np.float64) / half)
    lane = np.arange(HEAD_DIM)
    pos = np.where(lane[None, :] < half, (t // GRID_W)[:, None], (t % GRID_W)[:, None])
    ang = pos * freqs[lane % (half // 2)][None, :]
    sign = np.where((lane % half) < half // 2, -1.0, 1.0)[None, :]
    return (jnp.asarray(np.cos(ang), F32), jnp.asarray(np.sin(ang) * sign, F32))


def kernel(x_prompt, x_sample, c, cache_a_k, cache_a_v, cache_b_k, cache_b_v, c_ctx, w_mod, b_mod,
           g_attn_pre, g_attn_post, g_ffn_pre, g_ffn_post, w_in, rpb, g_qnorm, g_knorm, w_out,
           w_up, conv_w, conv_b, w_down):
    batch, seq, _ = x_prompt.shape
    dec_batch, dec_seq, _ = x_sample.shape
    depth = w_mod.shape[0]
    assert depth == 1 and dec_seq == GRID_H * GRID_W

    xp = x_prompt.reshape(batch * seq, D_MODEL)
    xs = x_sample.reshape(dec_batch * dec_seq, D_MODEL)
    l = 0
    conds = jnp.concatenate(
        [c_ctx[None, :], c, jnp.zeros((16 - 1 - dec_batch, D_MODEL), F32)], axis=0)
    mods = _modulation(conds, w_mod[l], b_mod[l]).reshape(16, 6, D_MODEL)
    mod_ctx = mods[0:1]
    mod_lat = mods[1:1 + dec_batch]

    g_pre = g_attn_pre[l].reshape(1, D_MODEL)
    g_post = g_attn_post[l].reshape(1, D_MODEL)
    g_fpre = g_ffn_pre[l].reshape(1, D_MODEL)
    g_fpost = g_ffn_post[l].reshape(1, D_MODEL)
    g_q = g_qnorm[l].reshape(1, HEAD_DIM)
    g_k = g_knorm[l].reshape(1, HEAD_DIM)
    conv_p = jnp.concatenate([conv_w[l], conv_b[l].reshape(1, 2 * D_FF)], axis=0)

    bias, w_in_bf = _bias_table(rpb[l], w_in[l])

    qkvh_c, st_ak, st_av, st_bk, st_bv, w_out_bf = _qkv(
        xp, mod_ctx, g_pre, w_in_bf, g_q, g_k, is_ctx=True, seq=seq, casts=[(w_out[l], (0, 1))])
    o_c = _attn_ctx(qkvh_c, seq=seq)
    x1_c, h2_c = _proj(o_c, o_c, xp, mod_ctx, w_out_bf, g_post, g_fpre, tiles_per_mod=1)

    qkvh_s, w_gate_bf = _qkv(xs, mod_lat, g_pre, w_in_bf, g_q, g_k, is_ctx=False, seq=dec_seq,
                             rope_tabs=_rope_tables(dec_seq), casts=[(w_up[l], (1, 2))])
    o_sa, w_val_bf = _attn_lat_a(qkvh_s, cache_a_k, cache_a_v, bias, w_up[l], (0, 2))
    o_sb, w_down_bf = _attn_lat_b(qkvh_s, cache_b_k, cache_b_v, w_down[l])
    x1_s, h2_s = _proj(o_sa, o_sb, xs, mod_lat, w_out_bf, g_post, g_fpre,
                       tiles_per_mod=dec_seq // 512)

    y_c = _ffn(h2_c, x1_c, mod_ctx, w_val_bf, w_gate_bf, conv_p, w_down_bf, g_fpost, seq=seq, tm=1024)
    y_s = _ffn(h2_s, x1_s, mod_lat, w_val_bf, w_gate_bf, conv_p, w_down_bf, g_fpost,
               seq=dec_seq, tm=1024)

    return (y_c.reshape(batch, seq, D_MODEL), y_s.reshape(dec_batch, dec_seq, D_MODEL),
            st_ak, st_av, st_bk, st_bv)
```

```python
import functools

import numpy as np
import jax
import jax.numpy as jnp
from jax import lax
from jax.experimental import pallas as pl
from jax.experimental.pallas import tpu as pltpu

F32 = jnp.float32
BF16 = jnp.bfloat16

D_MODEL = 2048
HEAD_DIM = 128
N_HEADS_A = 8
N_HEADS_B = 8
N_KV_B = 2
GRID_W = 64
GRID_H = 16
WIN_H = 8
WIN_W = 16
D_FF = 5632
ROPE_THETA = 10000.0
EPS = 1e-6
NEG_INF = -1e30
Q_SCALE = HEAD_DIM ** -0.5
IN_COLS = 4608
N_SLOTS = IN_COLS // HEAD_DIM
QKV_TN = 512
N_QKV_TILES = IN_COLS // QKV_TN
FFN_TF = 512
FFN_CW = 256
N_FFN_TILES = D_FF // FFN_TF
FIN_ROWS = 128
PROJ_ROWS = 128
ATT_ROWS = 128
HALF_Q = 512
BAND_K = 768
VMEM_LIMIT = 60 * 1024 * 1024


def _cparams(sem):
    return pltpu.CompilerParams(dimension_semantics=sem, vmem_limit_bytes=VMEM_LIMIT)


def _rms(x):
    return x * lax.rsqrt(jnp.mean(x * x, axis=-1, keepdims=True) + EPS)


def _mod_kernel(c_ref, w_ref, b_ref, o_ref):
    c = c_ref[...]
    a = (c / (1.0 + jnp.exp(-c))).astype(BF16)
    o_ref[...] = jnp.dot(a, w_ref[...].astype(BF16), preferred_element_type=F32) + b_ref[...]


def _modulation(conds, w_mod, b_mod):
    tn = 2048
    n = w_mod.shape[1]
    return pl.pallas_call(
        _mod_kernel,
        grid=(n // tn,),
        in_specs=[
            pl.BlockSpec((16, D_MODEL), lambda j: (0, 0)),
            pl.BlockSpec((D_MODEL, tn), lambda j: (0, j)),
            pl.BlockSpec((1, tn), lambda j: (0, j)),
        ],
        out_specs=pl.BlockSpec((16, tn), lambda j: (0, j)),
        out_shape=jax.ShapeDtypeStruct((16, n), F32),
        compiler_params=_cparams(("arbitrary",)),
        name="modulation",
    )(conds, w_mod, b_mod.reshape(1, n))


def _bias_kernel(rpb_ref, wf_ref, o_ref, wb_ref):
    _cast_slabs([wf_ref], [wb_ref])
    h = pl.program_id(0)
    qc = lax.broadcasted_iota(jnp.int32, (GRID_W, 128), 0)
    lane = lax.broadcasted_iota(jnp.int32, (GRID_W, 128), 1)
    kc = lane & (GRID_W - 1)
    dmat = kc - qc + (WIN_W - 1)
    cs = jnp.clip(qc - WIN_W // 2, 0, GRID_W - WIN_W)
    col_valid = (kc >= cs) & (kc < cs + WIN_W)
    neg = jnp.full((GRID_W, 128), NEG_INF, F32)

    g = []
    for dr in range(2 * WIN_H - 1):
        acc = neg
        for d in range(2 * WIN_W - 1):
            acc = jnp.where(dmat == d, rpb_ref[h, dr * (2 * WIN_W - 1) + d], acc)
        g.append(jnp.where(col_valid, acc, neg))

    def row_tile(r, kr):
        rs = min(max(r - WIN_H // 2, 0), GRID_H - WIN_H)
        if rs <= kr < rs + WIN_H:
            return g[kr - r + WIN_H - 1]
        return neg

    for half in range(2):
        for lr in range(HALF_Q // GRID_W):
            r = 8 * half + lr
            for p in range(BAND_K // 128):
                kr0 = 4 * half + 2 * p
                tile = jnp.where(lane < GRID_W, row_tile(r, kr0), row_tile(r, kr0 + 1))
                o_ref[0, half, lr * GRID_W:(lr + 1) * GRID_W, p * 128:(p + 1) * 128] = tile


def _bias_table(rpb_l, cast_w):
    rpb2 = rpb_l.reshape(N_HEADS_A, (2 * WIN_H - 1) * (2 * WIN_W - 1))
    w_in_spec, w_out_spec, w_shape = _cast_specs(cast_w, N_HEADS_A)
    return pl.pallas_call(
        _bias_kernel,
        grid=(N_HEADS_A,),
        in_specs=[pl.BlockSpec(memory_space=pltpu.SMEM), w_in_spec],
        out_specs=[pl.BlockSpec((1, 2, HALF_Q, BAND_K), lambda h: (h, 0, 0, 0)), w_out_spec],
        out_shape=[jax.ShapeDtypeStruct((N_HEADS_A, 2, HALF_Q, BAND_K), F32), w_shape],
        compiler_params=_cparams(("arbitrary",)),
        name="bias_table",
    )(rpb2, cast_w)


def _cast_specs(w, steps, *, col_block=(0, 1), grid_index=lambda i: i):
    k, n = col_block
    rows = w.shape[0] // steps
    cols = w.shape[1] // n
    assert rows * steps == w.shape[0] and rows % 16 == 0 and cols * n == w.shape[1]
    in_spec = pl.BlockSpec((rows, cols), lambda *g: (grid_index(*g), k))
    out_spec = pl.BlockSpec((rows, cols), lambda *g: (grid_index(*g), 0))
    return in_spec, out_spec, jax.ShapeDtypeStruct((w.shape[0], cols), BF16)


def _cast_slabs(cast_in, cast_out):
    for wf_ref, wb_ref in zip(cast_in, cast_out):
        wb_ref[...] = wf_ref[...].astype(BF16)


def _rope(y, cos, sin_signed):
    lane = lax.broadcasted_iota(jnp.int32, y.shape, 1)
    swapped = jnp.where((lane & 63) < 32, pltpu.roll(y, 96, 1), pltpu.roll(y, 32, 1))
    return y * cos + swapped * sin_signed


def _qkv_kernel(*refs, is_ctx, tm, seq, n_cast):
    n_in = (6 if is_ctx else 8) + n_cast
    n_out = 5 if is_ctx else 1
    _cast_slabs(refs[n_in - n_cast:n_in], refs[n_in + n_out:n_in + n_out + n_cast])
    if is_ctx:
        x_ref, mod_ref, gpre_ref, w_ref, gq_ref, gk_ref = refs[:6]
        o_ref, sak_ref, sav_ref, sbk_ref, sbv_ref = refs[n_in:n_in + 5]
        out_ref = refs[-1]
    else:
        x_ref, mod_ref, gpre_ref, w_ref, gq_ref, gk_ref, cos_ref, sin_ref = refs[:8]
        out_ref = refs[n_in]
    nb = max(tm // seq, 1)
    rows_per_state = min(seq, tm)
    heads_per_tile = QKV_TN // HEAD_DIM
    sh1 = mod_ref[0, 0:1, :]
    sc1 = mod_ref[0, 1:2, :]
    h = (_rms(x_ref[...]) * (gpre_ref[...] * (1.0 + sc1)) + sh1).astype(BF16)
    gq = gq_ref[...]
    gk = gk_ref[...]

    def project(n):
        return jnp.dot(h, w_ref[:, n * QKV_TN:(n + 1) * QKV_TN], preferred_element_type=F32)

    def store_state(s_ref, slot, val):
        for bb in range(nb):
            s_ref[bb, 0, slot] = val[bb * rows_per_state:(bb + 1) * rows_per_state]

    def rope(y):
        if is_ctx:
            return y
        return _rope(y, cos_ref[...], sin_ref[...])

    def epilogue(n, res):
        for hh in range(heads_per_tile):
            slot = n * heads_per_tile + hh
            v = res[:, hh * HEAD_DIM:(hh + 1) * HEAD_DIM]
            if slot < 8:
                out_ref[slot] = (v * Q_SCALE).astype(BF16)
            elif slot < 24:
                out_ref[slot] = v.astype(BF16)
                if is_ctx:
                    store_state(sak_ref if slot < 16 else sav_ref, slot % 8, v)
            elif slot < 32:
                out_ref[slot] = (rope(_rms(v) * gq) * Q_SCALE).astype(BF16)
            elif slot < 34:
                y = _rms(v) * gk
                if is_ctx:
                    store_state(sbk_ref, slot - 32, y)
                out_ref[slot] = rope(y).astype(BF16)
            else:
                out_ref[slot] = v.astype(BF16)
                if is_ctx:
                    store_state(sbv_ref, slot - 34, v)

    order = [6, 7, 8, 0, 1, 2, 3, 4, 5]
    res = project(order[0])
    for pos, n in enumerate(order):
        nxt = project(order[pos + 1]) if pos + 1 < len(order) else None
        epilogue(n, res)
        res = nxt
        if is_ctx and n == 8:
            _ctx_attn_b(out_ref, o_ref, nb=nb, seq=seq)
    if is_ctx:
        _ctx_attn_a(out_ref, o_ref, nb=nb, seq=seq)


def _qkv(x2d, mods, g_pre, w_in_bf, g_q, g_k, *, is_ctx, seq, rope_tabs=None, casts=()):
    t = x2d.shape[0]
    tm = 512
    nb = max(tm // seq, 1)
    tiles_per_seq = max(seq // tm, 1)
    nmod = mods.shape[0]
    mod_idx = (lambda i: (0, 0, 0)) if nmod == 1 else (lambda i: (i // tiles_per_seq, 0, 0))
    in_specs = [
        pl.BlockSpec((tm, D_MODEL), lambda i: (i, 0)),
        pl.BlockSpec((1, 6, D_MODEL), mod_idx),
        pl.BlockSpec((1, D_MODEL), lambda i: (0, 0)),
        pl.BlockSpec((D_MODEL, IN_COLS), lambda i: (0, 0), pipeline_mode=pl.Buffered(1)),
        pl.BlockSpec((1, HEAD_DIM), lambda i: (0, 0)),
        pl.BlockSpec((1, HEAD_DIM), lambda i: (0, 0)),
    ]
    args = [x2d, mods, g_pre, w_in_bf, g_q, g_k]
    out_specs = [pl.BlockSpec((N_SLOTS, tm, HEAD_DIM), lambda i: (0, i, 0))]
    out_shape = [jax.ShapeDtypeStruct((N_SLOTS, t, HEAD_DIM), BF16)]
    scratch = []
    if is_ctx:
        assert tm % seq == 0
        out_specs = [pl.BlockSpec((tm, D_MODEL), lambda i: (i, 0))]
        out_shape = [jax.ShapeDtypeStruct((t, D_MODEL), BF16)]
        scratch = [pltpu.VMEM((N_SLOTS, tm, HEAD_DIM), BF16)]
        nbatch = t // seq
        for nh in (N_HEADS_A, N_HEADS_A, N_KV_B, N_KV_B):
            out_specs.append(pl.BlockSpec((nb, 1, nh, seq, HEAD_DIM), lambda i: (i, 0, 0, 0, 0)))
            out_shape.append(jax.ShapeDtypeStruct((nbatch, 1, nh, seq, HEAD_DIM), F32))
    else:
        rope_idx = lambda i: (i % tiles_per_seq, 0)
        in_specs += [pl.BlockSpec((tm, HEAD_DIM), rope_idx), pl.BlockSpec((tm, HEAD_DIM), rope_idx)]
        args += list(rope_tabs)
    for w, col_block in casts:
        w_in_spec, w_out_spec, w_shape = _cast_specs(w, t // tm, col_block=col_block)
        in_specs.append(w_in_spec)
        args.append(w)
        out_specs.append(w_out_spec)
        out_shape.append(w_shape)
    return pl.pallas_call(
        functools.partial(_qkv_kernel, is_ctx=is_ctx, tm=tm, seq=seq, n_cast=len(casts)),
        grid=(t // tm,),
        in_specs=in_specs,
        out_specs=out_specs,
        out_shape=out_shape,
        scratch_shapes=scratch,
        compiler_params=_cparams(("arbitrary",)),
        name="qkv_ctx" if is_ctx else "qkv_lat",
    )(*args)


def _qk(q, k):
    return lax.dot_general(q, k, (((1,), (1,)), ((), ())), preferred_element_type=F32)


def _with_ones(v):
    return jnp.concatenate([v, jnp.ones_like(v)], axis=1)


def _softmax_pv(scores, values):
    m = functools.reduce(jnp.maximum, [jnp.max(s, axis=-1, keepdims=True) for s in scores])
    with_ones = values[0].shape[1] == 2 * HEAD_DIM
    acc = None
    den = None
    for s, v in zip(scores, values):
        e = jnp.exp(s - m)
        o = jnp.dot(e.astype(BF16), v, preferred_element_type=F32)
        acc = o if acc is None else acc + o
        if not with_ones:
            l = jnp.sum(e, axis=-1, keepdims=True)
            den = l if den is None else den + l
    if with_ones:
        return acc[:, :HEAD_DIM] / acc[:, HEAD_DIM:HEAD_DIM + 1]
    return acc / den


def _ctx_attn_a(qkv_ref, o_ref, *, nb, seq):
    for bb in range(nb):
        rows = slice(bb * seq, (bb + 1) * seq)
        for h in range(N_HEADS_A):
            q = qkv_ref[h, rows, :]
            k = qkv_ref[N_HEADS_A + h, rows, :]
            v = qkv_ref[2 * N_HEADS_A + h, rows, :]
            o = _softmax_pv([_qk(q, k)], [v])
            o_ref[rows, h * HEAD_DIM:(h + 1) * HEAD_DIM] = o.astype(BF16)


def _ctx_attn_b(qkv_ref, o_ref, *, nb, seq):
    group = N_HEADS_B // N_KV_B
    for bb in range(nb):
        rows = slice(bb * seq, (bb + 1) * seq)
        for j in range(N_KV_B):
            k = qkv_ref[32 + j, rows, :]
            v = qkv_ref[34 + j, rows, :]
            q = jnp.concatenate([qkv_ref[24 + j * group + g, rows, :] for g in range(group)], axis=0)
            o = _softmax_pv([_qk(q, k)], [v])
            for g in range(group):
                c0 = (N_HEADS_A + j * group + g) * HEAD_DIM
                o_ref[rows, c0:c0 + HEAD_DIM] = o[g * seq:(g + 1) * seq].astype(BF16)


def _attn_lat_a_kernel(q_ref, k0_ref, k1_ref, k2_ref, v0_ref, v1_ref, v2_ref,
                       ck_ref, cv_ref, bias_ref, wf_ref, o_ref, wb_ref):
    _cast_slabs([wf_ref], [wb_ref])
    for h in range(N_HEADS_A):
        ck = ck_ref[0, 0, h].astype(BF16)
        v_nb = _with_ones(jnp.concatenate([v0_ref[h], v1_ref[h], v2_ref[h]], axis=0))
        cv = _with_ones(cv_ref[0, 0, h].astype(BF16))
        for r in range(0, HALF_Q, ATT_ROWS):
            rows = slice(r, r + ATT_ROWS)
            q = q_ref[h, rows, :]
            s_nb = jnp.concatenate([_qk(q, k0_ref[h]), _qk(q, k1_ref[h]), _qk(q, k2_ref[h])], axis=1)
            s_nb = s_nb + bias_ref[h, 0, rows, :]
            o = _softmax_pv([s_nb, _qk(q, ck)], [v_nb, cv])
            o_ref[rows, h * HEAD_DIM:(h + 1) * HEAD_DIM] = o.astype(BF16)


def _attn_lat_a(qkvh, cache_k, cache_v, bias, cast_w, cast_cols):
    t = qkvh.shape[1]
    nbatch = t // 1024
    kq = 256
    w_in_spec, w_out_spec, w_shape = _cast_specs(cast_w, 2 * nbatch, col_block=cast_cols,
                                                  grid_index=lambda hf, b: hf * nbatch + b)

    def kspec(slot_blk, j):
        return pl.BlockSpec((8, kq, HEAD_DIM), lambda hf, b: (slot_blk, b * 4 + hf + j, 0))

    cache_spec = pl.BlockSpec((1, 1, N_HEADS_A, 256, HEAD_DIM), lambda hf, b: (b, 0, 0, 0, 0))
    return pl.pallas_call(
        _attn_lat_a_kernel,
        grid=(2, nbatch),
        in_specs=[
            pl.BlockSpec((8, HALF_Q, HEAD_DIM), lambda hf, b: (0, b * 2 + hf, 0)),
            kspec(1, 0), kspec(1, 1), kspec(1, 2),
            kspec(2, 0), kspec(2, 1), kspec(2, 2),
            cache_spec, cache_spec,
            pl.BlockSpec((N_HEADS_A, 1, HALF_Q, BAND_K), lambda hf, b: (0, hf, 0, 0)),
            w_in_spec,
        ],
        out_specs=[pl.BlockSpec((HALF_Q, N_HEADS_A * HEAD_DIM), lambda hf, b: (b * 2 + hf, 0)),
                   w_out_spec],
        out_shape=[jax.ShapeDtypeStruct((t, N_HEADS_A * HEAD_DIM), BF16), w_shape],
        compiler_params=_cparams(("arbitrary", "arbitrary")),
        name="attn_lat_a",
    )(qkvh, qkvh, qkvh, qkvh, qkvh, qkvh, qkvh, cache_k, cache_v, bias, cast_w)


def _attn_lat_b_kernel(q_ref, k_ref, v_ref, ck_ref, cv_ref, wf_ref, o_ref, wb_ref):
    _cast_slabs([wf_ref], [wb_ref])
    k = k_ref[0]
    v = _with_ones(v_ref[0])
    ck = ck_ref[0, 0, 0].astype(BF16)
    cv = _with_ones(cv_ref[0, 0, 0].astype(BF16))
    for g in range(N_HEADS_B // N_KV_B):
        for r in range(0, q_ref.shape[1], ATT_ROWS):
            rows = slice(r, r + ATT_ROWS)
            q = q_ref[g, rows, :]
            o = _softmax_pv([_qk(q, k), _qk(q, ck)], [v, cv])
            o_ref[rows, g * HEAD_DIM:(g + 1) * HEAD_DIM] = o.astype(BF16)


def _attn_lat_b(qkvh, cache_k, cache_v, cast_w):
    t = qkvh.shape[1]
    nbatch = t // 1024
    cache_spec = pl.BlockSpec((1, 1, 1, 256, HEAD_DIM), lambda b, j: (b, 0, j, 0, 0))
    w_in_spec, w_out_spec, w_shape = _cast_specs(cast_w, nbatch * N_KV_B,
                                                  grid_index=lambda b, j: b * N_KV_B + j)
    return pl.pallas_call(
        _attn_lat_b_kernel,
        grid=(nbatch, N_KV_B),
        in_specs=[
            pl.BlockSpec((4, 1024, HEAD_DIM), lambda b, j: (6 + j, b, 0)),
            pl.BlockSpec((1, 1024, HEAD_DIM), lambda b, j: (32 + j, b, 0)),
            pl.BlockSpec((1, 1024, HEAD_DIM), lambda b, j: (34 + j, b, 0)),
            cache_spec, cache_spec, w_in_spec,
        ],
        out_specs=[pl.BlockSpec((1024, 512), lambda b, j: (b, j)), w_out_spec],
        out_shape=[jax.ShapeDtypeStruct((t, N_HEADS_B * HEAD_DIM), BF16), w_shape],
        compiler_params=_cparams(("arbitrary", "arbitrary")),
        name="attn_lat_b",
    )(qkvh, qkvh, qkvh, cache_k, cache_v, cast_w)


def _proj_kernel(oa_ref, ob_ref, x_ref, mod_ref, w_ref, gpost_ref, gffn_ref, x1_ref, h2_ref):
    gain1 = mod_ref[0, 2:3, :] * gpost_ref[...]
    sh2 = mod_ref[0, 3:4, :]
    gain2 = gffn_ref[...] * (1.0 + mod_ref[0, 4:5, :])
    tm, ka = oa_ref.shape
    chunks = [slice(r, r + PROJ_ROWS) for r in range(0, tm, PROJ_ROWS)]

    def project(rows):
        return (jnp.dot(oa_ref[rows, :], w_ref[:ka, :], preferred_element_type=F32)
                + jnp.dot(ob_ref[rows, :], w_ref[ka:, :], preferred_element_type=F32))

    t = project(chunks[0])
    for r, rows in enumerate(chunks):
        nxt = project(chunks[r + 1]) if r + 1 < len(chunks) else None
        x1 = x_ref[rows, :] + gain1 * _rms(t)
        x1_ref[rows, :] = x1
        h2_ref[rows, :] = (_rms(x1) * gain2 + sh2).astype(BF16)
        t = nxt


def _proj(o_a, o_b, x2d, mods, w_out_bf, g_post, g_ffn, *, tiles_per_mod):
    t = x2d.shape[0]
    tm = 512
    nmod = mods.shape[0]
    mod_idx = (lambda i: (0, 0, 0)) if nmod == 1 else (lambda i: (i // tiles_per_mod, 0, 0))
    ka = D_MODEL // 2
    ob_idx = (lambda i: (i, 1)) if o_b.shape[1] == D_MODEL else (lambda i: (i, 0))
    in_specs = [
        pl.BlockSpec((tm, ka), lambda i: (i, 0)),
        pl.BlockSpec((tm, ka), ob_idx),
        pl.BlockSpec((tm, D_MODEL), lambda i: (i, 0)),
        pl.BlockSpec((1, 6, D_MODEL), mod_idx),
        pl.BlockSpec((D_MODEL, D_MODEL), lambda i: (0, 0), pipeline_mode=pl.Buffered(1)),
        pl.BlockSpec((1, D_MODEL), lambda i: (0, 0)),
        pl.BlockSpec((1, D_MODEL), lambda i: (0, 0)),
    ]
    out_specs = [
        pl.BlockSpec((tm, D_MODEL), lambda i: (i, 0)),
        pl.BlockSpec((tm, D_MODEL), lambda i: (i, 0)),
    ]
    out_shape = [
        jax.ShapeDtypeStruct((t, D_MODEL), F32),
        jax.ShapeDtypeStruct((t, D_MODEL), BF16),
    ]
    return pl.pallas_call(
        _proj_kernel,
        grid=(t // tm,),
        in_specs=in_specs,
        out_specs=out_specs,
        out_shape=out_shape,
        compiler_params=_cparams(("arbitrary",)),
        name="proj",
    )(o_a, o_b, x2d, mods, w_out_bf, g_post, g_ffn)


def _ffn_kernel(h2_ref, x1_ref, mod_ref, wv_ref, wg_ref, cpv_ref, cpg_ref,
                wd_ref, gpost_ref, out_ref, x1_buf, x1_sem, *u_scr, tm, seq):
    i = pl.program_id(0)
    j = pl.program_id(1)
    nseq = tm // seq
    stride = seq + 8
    chunks = [slice(c * FFN_CW, (c + 1) * FFN_CW) for c in range(FFN_TF // FFN_CW)]

    def x1_copy():
        return pltpu.make_async_copy(x1_ref.at[pl.ds(pl.multiple_of(i * tm, tm), tm), :], x1_buf, x1_sem)

    def conv(scr, base, cp_ref):
        lo = scr[base - 1:base - 1 + seq, :]
        mid = scr[base:base + seq, :]
        hi = scr[base + 1:base + 1 + seq, :]
        return (mid * cp_ref[1:2, :] + cp_ref[3:4, :]) + lo * cp_ref[0:1, :] + hi * cp_ref[2:3, :]

    def step(first):
        h2 = h2_ref[...]
        for c, cols in enumerate(chunks):
            for k, w_ref in enumerate((wv_ref, wg_ref)):
                u = jnp.dot(h2, w_ref[:, cols], preferred_element_type=F32)
                for s in range(nseq):
                    u_scr[2 * c + k][8 + s * stride:8 + s * stride + seq, :] = u[s * seq:(s + 1) * seq]
        for c, cols in enumerate(chunks):
            acts = []
            for s in range(nseq):
                base = 8 + s * stride
                val = conv(u_scr[2 * c], base, cpv_ref.at[:, cols])
                gate = conv(u_scr[2 * c + 1], base, cpg_ref.at[:, cols])
                acts.append((gate / (1.0 + jnp.exp(-gate)) * val).astype(BF16))
            act = acts[0] if nseq == 1 else jnp.concatenate(acts, axis=0)
            d = jnp.dot(act, wd_ref[cols, :], preferred_element_type=F32)
            if first and c == 0:
                out_ref[...] = d
            else:
                out_ref[...] += d

    @pl.when(j == 0)
    def _():
        x1_copy().start()
        for scr in u_scr:
            for s in range(nseq + 1):
                scr[s * stride:s * stride + 8, :] = jnp.zeros((8, FFN_CW), F32)
        step(True)

    @pl.when(j > 0)
    def _():
        step(False)

    @pl.when(j == N_FFN_TILES - 1)
    def _():
        x1_copy().wait()
        gain = mod_ref[0, 5:6, :] * gpost_ref[...]
        for r in range(0, tm, FIN_ROWS):
            rows = slice(r, r + FIN_ROWS)
            out_ref[rows, :] = x1_buf[rows, :] + gain * _rms(out_ref[rows, :])


def _ffn(h2, x1, mods, w_val_bf, w_gate_bf, conv_p, w_down_bf, g_post, *, seq, tm):
    t = x1.shape[0]
    nmod = mods.shape[0]
    mod_idx = (lambda i, j: (0, 0, 0)) if nmod == 1 else (lambda i, j: (i * tm // seq, 0, 0))
    nj = N_FFN_TILES
    u_rows = 8 + (tm // seq) * (seq + 8)
    return pl.pallas_call(
        functools.partial(_ffn_kernel, tm=tm, seq=seq),
        grid=(t // tm, nj),
        in_specs=[
            pl.BlockSpec((tm, D_MODEL), lambda i, j: (i, 0)),
            pl.BlockSpec(memory_space=pl.ANY),
            pl.BlockSpec((1, 6, D_MODEL), mod_idx),
            pl.BlockSpec((D_MODEL, FFN_TF), lambda i, j: (0, j)),
            pl.BlockSpec((D_MODEL, FFN_TF), lambda i, j: (0, j)),
            pl.BlockSpec((4, FFN_TF), lambda i, j: (0, j)),
            pl.BlockSpec((4, FFN_TF), lambda i, j: (0, nj + j)),
            pl.BlockSpec((FFN_TF, D_MODEL), lambda i, j: (j, 0)),
            pl.BlockSpec((1, D_MODEL), lambda i, j: (0, 0)),
        ],
        out_specs=pl.BlockSpec((tm, D_MODEL), lambda i, j: (i, 0)),
        out_shape=jax.ShapeDtypeStruct((t, D_MODEL), F32),
        scratch_shapes=[pltpu.VMEM((tm, D_MODEL), F32), pltpu.SemaphoreType.DMA(())]
        + [pltpu.VMEM((u_rows, FFN_CW), F32) for _ in range(2 * (FFN_TF // FFN_CW))],
        compiler_params=_cparams(("arbitrary", "arbitrary")),
        name="ffn",
    )(h2, x1, mods, w_val_bf, w_gate_bf, conv_p, conv_p, w_down_bf, g_post)


def _rope_tables(n_tok):
    half = HEAD_DIM // 2
    t = np.arange(n_tok)
    freqs = ROPE_THETA ** (-np.arange(0, half, 2, dtype=np.float64) / half)
    lane = np.arange(HEAD_DIM)
    pos = np.where(lane[None, :] < half, (t // GRID_W)[:, None], (t % GRID_W)[:, None])
    ang = pos * freqs[lane % (half // 2)][None, :]
    sign = np.where((lane % half) < half // 2, -1.0, 1.0)[None, :]
    return (jnp.asarray(np.cos(ang), F32), jnp.asarray(np.sin(ang) * sign, F32))


def kernel(x_prompt, x_sample, c, cache_a_k, cache_a_v, cache_b_k, cache_b_v, c_ctx, w_mod, b_mod,
           g_attn_pre, g_attn_post, g_ffn_pre, g_ffn_post, w_in, rpb, g_qnorm, g_knorm, w_out,
           w_up, conv_w, conv_b, w_down):
    batch, seq, _ = x_prompt.shape
    dec_batch, dec_seq, _ = x_sample.shape
    depth = w_mod.shape[0]
    assert depth == 1 and dec_seq == GRID_H * GRID_W

    xp = x_prompt.reshape(batch * seq, D_MODEL)
    xs = x_sample.reshape(dec_batch * dec_seq, D_MODEL)
    l = 0
    conds = jnp.concatenate(
        [c_ctx[None, :], c, jnp.zeros((16 - 1 - dec_batch, D_MODEL), F32)], axis=0)
    mods = _modulation(conds, w_mod[l], b_mod[l]).reshape(16, 6, D_MODEL)
    mod_ctx = mods[0:1]
    mod_lat = mods[1:1 + dec_batch]

    g_pre = g_attn_pre[l].reshape(1, D_MODEL)
    g_post = g_attn_post[l].reshape(1, D_MODEL)
    g_fpre = g_ffn_pre[l].reshape(1, D_MODEL)
    g_fpost = g_ffn_post[l].reshape(1, D_MODEL)
    g_q = g_qnorm[l].reshape(1, HEAD_DIM)
    g_k = g_knorm[l].reshape(1, HEAD_DIM)
    conv_p = jnp.concatenate([conv_w[l], conv_b[l].reshape(1, 2 * D_FF)], axis=0)

    bias, w_in_bf = _bias_table(rpb[l], w_in[l])

    o_c, st_ak, st_av, st_bk, st_bv, w_out_bf = _qkv(
        xp, mod_ctx, g_pre, w_in_bf, g_q, g_k, is_ctx=True, seq=seq, casts=[(w_out[l], (0, 1))])
    x1_c, h2_c = _proj(o_c, o_c, xp, mod_ctx, w_out_bf, g_post, g_fpre, tiles_per_mod=1)

    qkvh_s, w_gate_bf = _qkv(xs, mod_lat, g_pre, w_in_bf, g_q, g_k, is_ctx=False, seq=dec_seq,
                             rope_tabs=_rope_tables(dec_seq), casts=[(w_up[l], (1, 2))])
    o_sa, w_val_bf = _attn_lat_a(qkvh_s, cache_a_k, cache_a_v, bias, w_up[l], (0, 2))
    o_sb, w_down_bf = _attn_lat_b(qkvh_s, cache_b_k, cache_b_v, w_down[l])
    x1_s, h2_s = _proj(o_sa, o_sb, xs, mod_lat, w_out_bf, g_post, g_fpre,
                       tiles_per_mod=dec_seq // 512)

    y_c = _ffn(h2_c, x1_c, mod_ctx, w_val_bf, w_gate_bf, conv_p, w_down_bf, g_fpost, seq=seq, tm=1024)
    y_s = _ffn(h2_s, x1_s, mod_lat, w_val_bf, w_gate_bf, conv_p, w_down_bf, g_fpost,
               seq=dec_seq, tm=1024)

    return (y_c.reshape(batch, seq, D_MODEL), y_s.reshape(dec_batch, dec_seq, D_MODEL),
            st_ak, st_av, st_bk, st_bv)
```

```python
import functools

import numpy as np
import jax
import jax.numpy as jnp
from jax import lax
from jax.experimental import pallas as pl
from jax.experimental.pallas import tpu as pltpu

F32 = jnp.float32
BF16 = jnp.bfloat16

D_MODEL = 2048
HEAD_DIM = 128
N_HEADS_A = 8
N_HEADS_B = 8
N_KV_B = 2
GRID_W = 64
GRID_H = 16
WIN_H = 8
WIN_W = 16
D_FF = 5632
ROPE_THETA = 10000.0
EPS = 1e-6
NEG_INF = -1e30
Q_SCALE = HEAD_DIM ** -0.5
IN_COLS = 4608
N_SLOTS = IN_COLS // HEAD_DIM
QKV_TN = 512
N_QKV_TILES = IN_COLS // QKV_TN
FFN_TF = 512
FFN_CW = 256
N_FFN_TILES = D_FF // FFN_TF
FIN_ROWS = 128
PROJ_ROWS = 128
ATT_ROWS = 128
HALF_Q = 512
BAND_K = 768
VMEM_LIMIT = 60 * 1024 * 1024


def _cparams(sem):
    return pltpu.CompilerParams(dimension_semantics=sem, vmem_limit_bytes=VMEM_LIMIT)


def _rms(x):
    return x * lax.rsqrt(jnp.mean(x * x, axis=-1, keepdims=True) + EPS)


def _mod_kernel(c_ref, w_ref, b_ref, o_ref):
    c = c_ref[...]
    a = (c / (1.0 + jnp.exp(-c))).astype(BF16)
    o_ref[...] = jnp.dot(a, w_ref[...].astype(BF16), preferred_element_type=F32) + b_ref[...]


def _modulation(conds, w_mod, b_mod):
    tn = 2048
    n = w_mod.shape[1]
    return pl.pallas_call(
        _mod_kernel,
        grid=(n // tn,),
        in_specs=[
            pl.BlockSpec((16, D_MODEL), lambda j: (0, 0)),
            pl.BlockSpec((D_MODEL, tn), lambda j: (0, j)),
            pl.BlockSpec((1, tn), lambda j: (0, j)),
        ],
        out_specs=pl.BlockSpec((16, tn), lambda j: (0, j)),
        out_shape=jax.ShapeDtypeStruct((16, n), F32),
        compiler_params=_cparams(("arbitrary",)),
        name="modulation",
    )(conds, w_mod, b_mod.reshape(1, n))


def _bias_kernel(rpb_ref, *refs):
    n_cast = (len(refs) - 1) // 2
    o_ref = refs[n_cast]
    _cast_slabs(refs[:n_cast], refs[n_cast + 1:])
    h = pl.program_id(0)
    qc = lax.broadcasted_iota(jnp.int32, (GRID_W, 128), 0)
    lane = lax.broadcasted_iota(jnp.int32, (GRID_W, 128), 1)
    kc = lane & (GRID_W - 1)
    dmat = kc - qc + (WIN_W - 1)
    cs = jnp.clip(qc - WIN_W // 2, 0, GRID_W - WIN_W)
    col_valid = (kc >= cs) & (kc < cs + WIN_W)
    neg = jnp.full((GRID_W, 128), NEG_INF, F32)

    g = []
    for dr in range(2 * WIN_H - 1):
        acc = neg
        for d in range(2 * WIN_W - 1):
            acc = jnp.where(dmat == d, rpb_ref[h, dr * (2 * WIN_W - 1) + d], acc)
        g.append(jnp.where(col_valid, acc, neg))

    def row_tile(r, kr):
        rs = min(max(r - WIN_H // 2, 0), GRID_H - WIN_H)
        if rs <= kr < rs + WIN_H:
            return g[kr - r + WIN_H - 1]
        return neg

    for half in range(2):
        for lr in range(HALF_Q // GRID_W):
            r = 8 * half + lr
            for p in range(BAND_K // 128):
                kr0 = 4 * half + 2 * p
                tile = jnp.where(lane < GRID_W, row_tile(r, kr0), row_tile(r, kr0 + 1))
                o_ref[0, half, lr * GRID_W:(lr + 1) * GRID_W, p * 128:(p + 1) * 128] = tile


def _bias_table(rpb_l, cast_ws):
    rpb2 = rpb_l.reshape(N_HEADS_A, (2 * WIN_H - 1) * (2 * WIN_W - 1))
    specs = [_cast_specs(w, N_HEADS_A) for w in cast_ws]
    return pl.pallas_call(
        _bias_kernel,
        grid=(N_HEADS_A,),
        in_specs=[pl.BlockSpec(memory_space=pltpu.SMEM)] + [sp[0] for sp in specs],
        out_specs=[pl.BlockSpec((1, 2, HALF_Q, BAND_K), lambda h: (h, 0, 0, 0))] + [sp[1] for sp in specs],
        out_shape=[jax.ShapeDtypeStruct((N_HEADS_A, 2, HALF_Q, BAND_K), F32)] + [sp[2] for sp in specs],
        compiler_params=_cparams(("arbitrary",)),
        name="bias_table",
    )(rpb2, *cast_ws)


def _cast_specs(w, steps, *, col_block=(0, 1), grid_index=lambda i: i):
    k, n = col_block
    rows = w.shape[0] // steps
    cols = w.shape[1] // n
    assert rows * steps == w.shape[0] and rows % 16 == 0 and cols * n == w.shape[1]
    in_spec = pl.BlockSpec((rows, cols), lambda *g: (grid_index(*g), k))
    out_spec = pl.BlockSpec((rows, cols), lambda *g: (grid_index(*g), 0))
    return in_spec, out_spec, jax.ShapeDtypeStruct((w.shape[0], cols), BF16)


def _cast_slabs(cast_in, cast_out):
    for wf_ref, wb_ref in zip(cast_in, cast_out):
        wb_ref[...] = wf_ref[...].astype(BF16)


def _rope(y, cos, sin_signed):
    lane = lax.broadcasted_iota(jnp.int32, y.shape, 1)
    swapped = jnp.where((lane & 63) < 32, pltpu.roll(y, 96, 1), pltpu.roll(y, 32, 1))
    return y * cos + swapped * sin_signed


def _qkv_kernel(*refs, is_ctx, tm, seq, n_cast):
    n_in = (9 if is_ctx else 8) + n_cast
    n_out = 6 if is_ctx else 1
    _cast_slabs(refs[n_in - n_cast:n_in], refs[n_in + n_out:n_in + n_out + n_cast])
    if is_ctx:
        x_ref, mod_ref, gpre_ref, w_ref, gq_ref, gk_ref, wout_ref, gpost_ref, gffn_ref = refs[:9]
        x1_ref, h2_ref, sak_ref, sav_ref, sbk_ref, sbv_ref = refs[n_in:n_in + 6]
        out_ref, o_ref = refs[-2:]
    else:
        x_ref, mod_ref, gpre_ref, w_ref, gq_ref, gk_ref, cos_ref, sin_ref = refs[:8]
        out_ref = refs[n_in]
    nb = max(tm // seq, 1)
    rows_per_state = min(seq, tm)
    heads_per_tile = QKV_TN // HEAD_DIM
    sh1 = mod_ref[0, 0:1, :]
    sc1 = mod_ref[0, 1:2, :]
    h = (_rms(x_ref[...]) * (gpre_ref[...] * (1.0 + sc1)) + sh1).astype(BF16)
    gq = gq_ref[...]
    gk = gk_ref[...]

    def project(n):
        return jnp.dot(h, w_ref[:, n * QKV_TN:(n + 1) * QKV_TN], preferred_element_type=F32)

    def store_state(s_ref, slot, val):
        for bb in range(nb):
            s_ref[bb, 0, slot] = val[bb * rows_per_state:(bb + 1) * rows_per_state]

    def rope(y):
        if is_ctx:
            return y
        return _rope(y, cos_ref[...], sin_ref[...])

    def epilogue(n, res):
        for hh in range(heads_per_tile):
            slot = n * heads_per_tile + hh
            v = res[:, hh * HEAD_DIM:(hh + 1) * HEAD_DIM]
            if slot < 8:
                out_ref[slot] = (v * Q_SCALE).astype(BF16)
            elif slot < 24:
                out_ref[slot] = v.astype(BF16)
                if is_ctx:
                    store_state(sak_ref if slot < 16 else sav_ref, slot % 8, v)
            elif slot < 32:
                out_ref[slot] = (rope(_rms(v) * gq) * Q_SCALE).astype(BF16)
            elif slot < 34:
                y = _rms(v) * gk
                if is_ctx:
                    store_state(sbk_ref, slot - 32, y)
                out_ref[slot] = rope(y).astype(BF16)
            else:
                out_ref[slot] = v.astype(BF16)
                if is_ctx:
                    store_state(sbv_ref, slot - 34, v)

    order = [6, 7, 8, 0, 1, 2, 3, 4, 5]
    res = project(order[0])
    for pos, n in enumerate(order):
        nxt = project(order[pos + 1]) if pos + 1 < len(order) else None
        epilogue(n, res)
        res = nxt
        if is_ctx and n == 8:
            _ctx_attn_b(out_ref, o_ref, nb=nb, seq=seq)
    if is_ctx:
        _ctx_attn_a(out_ref, o_ref, nb=nb, seq=seq)
        _out_proj(o_ref, o_ref, x_ref, mod_ref, wout_ref, gpost_ref, gffn_ref, x1_ref, h2_ref)


def _qkv(x2d, mods, g_pre, w_in_bf, g_q, g_k, *, is_ctx, seq, rope_tabs=None, casts=(), out_proj=None):
    t = x2d.shape[0]
    tm = 256 if is_ctx else 512
    nb = max(tm // seq, 1)
    tiles_per_seq = max(seq // tm, 1)
    nmod = mods.shape[0]
    mod_idx = (lambda i: (0, 0, 0)) if nmod == 1 else (lambda i: (i // tiles_per_seq, 0, 0))
    in_specs = [
        pl.BlockSpec((tm, D_MODEL), lambda i: (i, 0)),
        pl.BlockSpec((1, 6, D_MODEL), mod_idx),
        pl.BlockSpec((1, D_MODEL), lambda i: (0, 0)),
        pl.BlockSpec((D_MODEL, IN_COLS), lambda i: (0, 0), pipeline_mode=pl.Buffered(1)),
        pl.BlockSpec((1, HEAD_DIM), lambda i: (0, 0)),
        pl.BlockSpec((1, HEAD_DIM), lambda i: (0, 0)),
    ]
    args = [x2d, mods, g_pre, w_in_bf, g_q, g_k]
    if is_ctx:
        w_out_bf, g_post, g_ffn = out_proj
        in_specs += [pl.BlockSpec((D_MODEL, D_MODEL), lambda i: (0, 0), pipeline_mode=pl.Buffered(1)),
                     pl.BlockSpec((1, D_MODEL), lambda i: (0, 0)), pl.BlockSpec((1, D_MODEL), lambda i: (0, 0))]
        args += [w_out_bf, g_post, g_ffn]
    out_specs = [pl.BlockSpec((N_SLOTS, tm, HEAD_DIM), lambda i: (0, i, 0))]
    out_shape = [jax.ShapeDtypeStruct((N_SLOTS, t, HEAD_DIM), BF16)]
    scratch = []
    if is_ctx:
        assert tm % seq == 0
        out_specs = [pl.BlockSpec((tm, D_MODEL), lambda i: (i, 0)), pl.BlockSpec((tm, D_MODEL), lambda i: (i, 0))]
        out_shape = [jax.ShapeDtypeStruct((t, D_MODEL), F32), jax.ShapeDtypeStruct((t, D_MODEL), BF16)]
        scratch = [pltpu.VMEM((N_SLOTS, tm, HEAD_DIM), BF16), pltpu.VMEM((tm, D_MODEL), BF16)]
        nbatch = t // seq
        for nh in (N_HEADS_A, N_HEADS_A, N_KV_B, N_KV_B):
            out_specs.append(pl.BlockSpec((nb, 1, nh, seq, HEAD_DIM), lambda i: (i, 0, 0, 0, 0)))
            out_shape.append(jax.ShapeDtypeStruct((nbatch, 1, nh, seq, HEAD_DIM), F32))
    else:
        rope_idx = lambda i: (i % tiles_per_seq, 0)
        in_specs += [pl.BlockSpec((tm, HEAD_DIM), rope_idx), pl.BlockSpec((tm, HEAD_DIM), rope_idx)]
        args += list(rope_tabs)
    for w, col_block in casts:
        w_in_spec, w_out_spec, w_shape = _cast_specs(w, t // tm, col_block=col_block)
        in_specs.append(w_in_spec)
        args.append(w)
        out_specs.append(w_out_spec)
        out_shape.append(w_shape)
    return pl.pallas_call(
        functools.partial(_qkv_kernel, is_ctx=is_ctx, tm=tm, seq=seq, n_cast=len(casts)),
        grid=(t // tm,),
        in_specs=in_specs,
        out_specs=out_specs,
        out_shape=out_shape,
        scratch_shapes=scratch,
        compiler_params=_cparams(("arbitrary",)),
        name="qkv_ctx" if is_ctx else "qkv_lat",
    )(*args)


def _qk(q, k):
    return lax.dot_general(q, k, (((1,), (1,)), ((), ())), preferred_element_type=F32)


def _with_ones(v):
    return jnp.concatenate([v, jnp.ones_like(v)], axis=1)


def _softmax_pv(scores, values):
    m = functools.reduce(jnp.maximum, [jnp.max(s, axis=-1, keepdims=True) for s in scores])
    with_ones = values[0].shape[1] == 2 * HEAD_DIM
    acc = None
    den = None
    for s, v in zip(scores, values):
        e = jnp.exp(s - m)
        o = jnp.dot(e.astype(BF16), v, preferred_element_type=F32)
        acc = o if acc is None else acc + o
        if not with_ones:
            l = jnp.sum(e, axis=-1, keepdims=True)
            den = l if den is None else den + l
    if with_ones:
        return acc[:, :HEAD_DIM] / acc[:, HEAD_DIM:HEAD_DIM + 1]
    return acc / den


def _ctx_attn_a(qkv_ref, o_ref, *, nb, seq):
    for bb in range(nb):
        rows = slice(bb * seq, (bb + 1) * seq)
        for h in range(N_HEADS_A):
            q = qkv_ref[h, rows, :]
            k = qkv_ref[N_HEADS_A + h, rows, :]
            v = qkv_ref[2 * N_HEADS_A + h, rows, :]
            o = _softmax_pv([_qk(q, k)], [v])
            o_ref[rows, h * HEAD_DIM:(h + 1) * HEAD_DIM] = o.astype(BF16)


def _ctx_attn_b(qkv_ref, o_ref, *, nb, seq):
    group = N_HEADS_B // N_KV_B
    for bb in range(nb):
        rows = slice(bb * seq, (bb + 1) * seq)
        for j in range(N_KV_B):
            k = qkv_ref[32 + j, rows, :]
            v = qkv_ref[34 + j, rows, :]
            q = jnp.concatenate([qkv_ref[24 + j * group + g, rows, :] for g in range(group)], axis=0)
            o = _softmax_pv([_qk(q, k)], [v])
            for g in range(group):
                c0 = (N_HEADS_A + j * group + g) * HEAD_DIM
                o_ref[rows, c0:c0 + HEAD_DIM] = o[g * seq:(g + 1) * seq].astype(BF16)


def _attn_lat_a_kernel(q_ref, k0_ref, k1_ref, k2_ref, v0_ref, v1_ref, v2_ref,
                       ck_ref, cv_ref, bias_ref, wf_ref, o_ref, wb_ref):
    _cast_slabs([wf_ref], [wb_ref])
    for h in range(N_HEADS_A):
        ck = ck_ref[0, 0, h].astype(BF16)
        v_nb = _with_ones(jnp.concatenate([v0_ref[h], v1_ref[h], v2_ref[h]], axis=0))
        cv = _with_ones(cv_ref[0, 0, h].astype(BF16))
        for r in range(0, HALF_Q, ATT_ROWS):
            rows = slice(r, r + ATT_ROWS)
            q = q_ref[h, rows, :]
            s_nb = jnp.concatenate([_qk(q, k0_ref[h]), _qk(q, k1_ref[h]), _qk(q, k2_ref[h])], axis=1)
            s_nb = s_nb + bias_ref[h, 0, rows, :]
            o = _softmax_pv([s_nb, _qk(q, ck)], [v_nb, cv])
            o_ref[rows, h * HEAD_DIM:(h + 1) * HEAD_DIM] = o.astype(BF16)


def _attn_lat_a(qkvh, cache_k, cache_v, bias, cast_w, cast_cols):
    t = qkvh.shape[1]
    nbatch = t // 1024
    kq = 256
    w_in_spec, w_out_spec, w_shape = _cast_specs(cast_w, 2 * nbatch, col_block=cast_cols,
                                                  grid_index=lambda hf, b: hf * nbatch + b)

    def kspec(slot_blk, j):
        return pl.BlockSpec((8, kq, HEAD_DIM), lambda hf, b: (slot_blk, b * 4 + hf + j, 0))

    cache_spec = pl.BlockSpec((1, 1, N_HEADS_A, 256, HEAD_DIM), lambda hf, b: (b, 0, 0, 0, 0))
    return pl.pallas_call(
        _attn_lat_a_kernel,
        grid=(2, nbatch),
        in_specs=[
            pl.BlockSpec((8, HALF_Q, HEAD_DIM), lambda hf, b: (0, b * 2 + hf, 0)),
            kspec(1, 0), kspec(1, 1), kspec(1, 2),
            kspec(2, 0), kspec(2, 1), kspec(2, 2),
            cache_spec, cache_spec,
            pl.BlockSpec((N_HEADS_A, 1, HALF_Q, BAND_K), lambda hf, b: (0, hf, 0, 0)),
            w_in_spec,
        ],
        out_specs=[pl.BlockSpec((HALF_Q, N_HEADS_A * HEAD_DIM), lambda hf, b: (b * 2 + hf, 0)),
                   w_out_spec],
        out_shape=[jax.ShapeDtypeStruct((t, N_HEADS_A * HEAD_DIM), BF16), w_shape],
        compiler_params=_cparams(("arbitrary", "arbitrary")),
        name="attn_lat_a",
    )(qkvh, qkvh, qkvh, qkvh, qkvh, qkvh, qkvh, cache_k, cache_v, bias, cast_w)


def _attn_lat_b_kernel(q_ref, k_ref, v_ref, ck_ref, cv_ref, wf_ref, o_ref, wb_ref):
    _cast_slabs([wf_ref], [wb_ref])
    k = k_ref[0]
    v = _with_ones(v_ref[0])
    ck = ck_ref[0, 0, 0].astype(BF16)
    cv = _with_ones(cv_ref[0, 0, 0].astype(BF16))
    for g in range(N_HEADS_B // N_KV_B):
        for r in range(0, q_ref.shape[1], ATT_ROWS):
            rows = slice(r, r + ATT_ROWS)
            q = q_ref[g, rows, :]
            o = _softmax_pv([_qk(q, k), _qk(q, ck)], [v, cv])
            o_ref[rows, g * HEAD_DIM:(g + 1) * HEAD_DIM] = o.astype(BF16)


def _attn_lat_b(qkvh, cache_k, cache_v, cast_w):
    t = qkvh.shape[1]
    nbatch = t // 1024
    cache_spec = pl.BlockSpec((1, 1, 1, 256, HEAD_DIM), lambda b, j: (b, 0, j, 0, 0))
    w_in_spec, w_out_spec, w_shape = _cast_specs(cast_w, nbatch * N_KV_B,
                                                  grid_index=lambda b, j: b * N_KV_B + j)
    return pl.pallas_call(
        _attn_lat_b_kernel,
        grid=(nbatch, N_KV_B),
        in_specs=[
            pl.BlockSpec((4, 1024, HEAD_DIM), lambda b, j: (6 + j, b, 0)),
            pl.BlockSpec((1, 1024, HEAD_DIM), lambda b, j: (32 + j, b, 0)),
            pl.BlockSpec((1, 1024, HEAD_DIM), lambda b, j: (34 + j, b, 0)),
            cache_spec, cache_spec, w_in_spec,
        ],
        out_specs=[pl.BlockSpec((1024, 512), lambda b, j: (b, j)), w_out_spec],
        out_shape=[jax.ShapeDtypeStruct((t, N_HEADS_B * HEAD_DIM), BF16), w_shape],
        compiler_params=_cparams(("arbitrary", "arbitrary")),
        name="attn_lat_b",
    )(qkvh, qkvh, qkvh, cache_k, cache_v, cast_w)


def _proj_kernel(oa_ref, ob_ref, x_ref, mod_ref, w_ref, gpost_ref, gffn_ref, x1_ref, h2_ref):
    _out_proj(oa_ref, ob_ref, x_ref, mod_ref, w_ref, gpost_ref, gffn_ref, x1_ref, h2_ref)


def _out_proj(oa_ref, ob_ref, x_ref, mod_ref, w_ref, gpost_ref, gffn_ref, x1_ref, h2_ref):
    gain1 = mod_ref[0, 2:3, :] * gpost_ref[...]
    sh2 = mod_ref[0, 3:4, :]
    gain2 = gffn_ref[...] * (1.0 + mod_ref[0, 4:5, :])
    tm = oa_ref.shape[0]
    ka = D_MODEL // 2
    kb = ob_ref.shape[1] - ka
    chunks = [slice(r, r + PROJ_ROWS) for r in range(0, tm, PROJ_ROWS)]

    def project(rows):
        return (jnp.dot(oa_ref[rows, :ka], w_ref[:ka, :], preferred_element_type=F32)
                + jnp.dot(ob_ref[rows, kb:], w_ref[ka:, :], preferred_element_type=F32))

    t = project(chunks[0])
    for r, rows in enumerate(chunks):
        nxt = project(chunks[r + 1]) if r + 1 < len(chunks) else None
        x1 = x_ref[rows, :] + gain1 * _rms(t)
        x1_ref[rows, :] = x1
        h2_ref[rows, :] = (_rms(x1) * gain2 + sh2).astype(BF16)
        t = nxt


def _proj(o_a, o_b, x2d, mods, w_out_bf, g_post, g_ffn, *, tiles_per_mod):
    t = x2d.shape[0]
    tm = 512
    nmod = mods.shape[0]
    mod_idx = (lambda i: (0, 0, 0)) if nmod == 1 else (lambda i: (i // tiles_per_mod, 0, 0))
    ka = D_MODEL // 2
    ob_idx = (lambda i: (i, 1)) if o_b.shape[1] == D_MODEL else (lambda i: (i, 0))
    in_specs = [
        pl.BlockSpec((tm, ka), lambda i: (i, 0)),
        pl.BlockSpec((tm, ka), ob_idx),
        pl.BlockSpec((tm, D_MODEL), lambda i: (i, 0)),
        pl.BlockSpec((1, 6, D_MODEL), mod_idx),
        pl.BlockSpec((D_MODEL, D_MODEL), lambda i: (0, 0), pipeline_mode=pl.Buffered(1)),
        pl.BlockSpec((1, D_MODEL), lambda i: (0, 0)),
        pl.BlockSpec((1, D_MODEL), lambda i: (0, 0)),
    ]
    out_specs = [
        pl.BlockSpec((tm, D_MODEL), lambda i: (i, 0)),
        pl.BlockSpec((tm, D_MODEL), lambda i: (i, 0)),
    ]
    out_shape = [
        jax.ShapeDtypeStruct((t, D_MODEL), F32),
        jax.ShapeDtypeStruct((t, D_MODEL), BF16),
    ]
    return pl.pallas_call(
        _proj_kernel,
        grid=(t // tm,),
        in_specs=in_specs,
        out_specs=out_specs,
        out_shape=out_shape,
        compiler_params=_cparams(("arbitrary",)),
        name="proj",
    )(o_a, o_b, x2d, mods, w_out_bf, g_post, g_ffn)


def _ffn_kernel(h2_ref, x1_ref, mod_ref, wv_ref, wg_ref, cpv_ref, cpg_ref,
                wd_ref, gpost_ref, out_ref, x1_buf, x1_sem, *u_scr, tm, seq):
    i = pl.program_id(0)
    j = pl.program_id(1)
    nseq = tm // seq
    stride = seq + 8
    chunks = [slice(c * FFN_CW, (c + 1) * FFN_CW) for c in range(FFN_TF // FFN_CW)]

    def x1_copy():
        return pltpu.make_async_copy(x1_ref.at[pl.ds(pl.multiple_of(i * tm, tm), tm), :], x1_buf, x1_sem)

    def conv(scr, base, cp_ref):
        lo = scr[base - 1:base - 1 + seq, :]
        mid = scr[base:base + seq, :]
        hi = scr[base + 1:base + 1 + seq, :]
        return (mid * cp_ref[1:2, :] + cp_ref[3:4, :]) + lo * cp_ref[0:1, :] + hi * cp_ref[2:3, :]

    def step(first):
        h2 = h2_ref[...]
        for c, cols in enumerate(chunks):
            for k, w_ref in enumerate((wv_ref, wg_ref)):
                u = jnp.dot(h2, w_ref[:, cols], preferred_element_type=F32)
                for s in range(nseq):
                    u_scr[2 * c + k][8 + s * stride:8 + s * stride + seq, :] = u[s * seq:(s + 1) * seq]
        for c, cols in enumerate(chunks):
            acts = []
            for s in range(nseq):
                base = 8 + s * stride
                val = conv(u_scr[2 * c], base, cpv_ref.at[:, cols])
                gate = conv(u_scr[2 * c + 1], base, cpg_ref.at[:, cols])
                acts.append((gate / (1.0 + jnp.exp(-gate)) * val).astype(BF16))
            act = acts[0] if nseq == 1 else jnp.concatenate(acts, axis=0)
            d = jnp.dot(act, wd_ref[cols, :], preferred_element_type=F32)
            if first and c == 0:
                out_ref[...] = d
            else:
                out_ref[...] += d

    @pl.when(j == 0)
    def _():
        x1_copy().start()
        for scr in u_scr:
            for s in range(nseq + 1):
                scr[s * stride:s * stride + 8, :] = jnp.zeros((8, FFN_CW), F32)
        step(True)

    @pl.when(j > 0)
    def _():
        step(False)

    @pl.when(j == N_FFN_TILES - 1)
    def _():
        x1_copy().wait()
        gain = mod_ref[0, 5:6, :] * gpost_ref[...]
        for r in range(0, tm, FIN_ROWS):
            rows = slice(r, r + FIN_ROWS)
            out_ref[rows, :] = x1_buf[rows, :] + gain * _rms(out_ref[rows, :])


def _ffn(h2, x1, mods, w_val_bf, w_gate_bf, conv_p, w_down_bf, g_post, *, seq, tm):
    t = x1.shape[0]
    nmod = mods.shape[0]
    mod_idx = (lambda i, j: (0, 0, 0)) if nmod == 1 else (lambda i, j: (i * tm // seq, 0, 0))
    nj = N_FFN_TILES
    u_rows = 8 + (tm // seq) * (seq + 8)
    return pl.pallas_call(
        functools.partial(_ffn_kernel, tm=tm, seq=seq),
        grid=(t // tm, nj),
        in_specs=[
            pl.BlockSpec((tm, D_MODEL), lambda i, j: (i, 0)),
            pl.BlockSpec(memory_space=pl.ANY),
            pl.BlockSpec((1, 6, D_MODEL), mod_idx),
            pl.BlockSpec((D_MODEL, FFN_TF), lambda i, j: (0, j)),
            pl.BlockSpec((D_MODEL, FFN_TF), lambda i, j: (0, j)),
            pl.BlockSpec((4, FFN_TF), lambda i, j: (0, j)),
            pl.BlockSpec((4, FFN_TF), lambda i, j: (0, nj + j)),
            pl.BlockSpec((FFN_TF, D_MODEL), lambda i, j: (j, 0)),
            pl.BlockSpec((1, D_MODEL), lambda i, j: (0, 0)),
        ],
        out_specs=pl.BlockSpec((tm, D_MODEL), lambda i, j: (i, 0)),
        out_shape=jax.ShapeDtypeStruct((t, D_MODEL), F32),
        scratch_shapes=[pltpu.VMEM((tm, D_MODEL), F32), pltpu.SemaphoreType.DMA(())]
        + [pltpu.VMEM((u_rows, FFN_CW), F32) for _ in range(2 * (FFN_TF // FFN_CW))],
        compiler_params=_cparams(("arbitrary", "arbitrary")),
        name="ffn",
    )(h2, x1, mods, w_val_bf, w_gate_bf, conv_p, conv_p, w_down_bf, g_post)


def _rope_tables(n_tok):
    half = HEAD_DIM // 2
    t = np.arange(n_tok)
    freqs = ROPE_THETA ** (-np.arange(0, half, 2, dtype=np.float64) / half)
    lane = np.arange(HEAD_DIM)
    pos = np.where(lane[None, :] < half, (t // GRID_W)[:, None], (t % GRID_W)[:, None])
    ang = pos * freqs[lane % (half // 2)][None, :]
    sign = np.where((lane % half) < half // 2, -1.0, 1.0)[None, :]
    return (jnp.asarray(np.cos(ang), F32), jnp.asarray(np.sin(ang) * sign, F32))


def kernel(x_prompt, x_sample, c, cache_a_k, cache_a_v, cache_b_k, cache_b_v, c_ctx, w_mod, b_mod,
           g_attn_pre, g_attn_post, g_ffn_pre, g_ffn_post, w_in, rpb, g_qnorm, g_knorm, w_out,
           w_up, conv_w, conv_b, w_down):
    batch, seq, _ = x_prompt.shape
    dec_batch, dec_seq, _ = x_sample.shape
    depth = w_mod.shape[0]
    assert depth == 1 and dec_seq == GRID_H * GRID_W

    xp = x_prompt.reshape(batch * seq, D_MODEL)
    xs = x_sample.reshape(dec_batch * dec_seq, D_MODEL)
    l = 0
    conds = jnp.concatenate(
        [c_ctx[None, :], c, jnp.zeros((16 - 1 - dec_batch, D_MODEL), F32)], axis=0)
    mods = _modulation(conds, w_mod[l], b_mod[l]).reshape(16, 6, D_MODEL)
    mod_ctx = mods[0:1]
    mod_lat = mods[1:1 + dec_batch]

    g_pre = g_attn_pre[l].reshape(1, D_MODEL)
    g_post = g_attn_post[l].reshape(1, D_MODEL)
    g_fpre = g_ffn_pre[l].reshape(1, D_MODEL)
    g_fpost = g_ffn_post[l].reshape(1, D_MODEL)
    g_q = g_qnorm[l].reshape(1, HEAD_DIM)
    g_k = g_knorm[l].reshape(1, HEAD_DIM)
    conv_p = jnp.concatenate([conv_w[l], conv_b[l].reshape(1, 2 * D_FF)], axis=0)

    bias, w_in_bf, w_out_bf = _bias_table(rpb[l], [w_in[l], w_out[l]])

    x1_c, h2_c, st_ak, st_av, st_bk, st_bv = _qkv(
        xp, mod_ctx, g_pre, w_in_bf, g_q, g_k, is_ctx=True, seq=seq, out_proj=(w_out_bf, g_post, g_fpre))

    qkvh_s, w_gate_bf = _qkv(xs, mod_lat, g_pre, w_in_bf, g_q, g_k, is_ctx=False, seq=dec_seq,
                             rope_tabs=_rope_tables(dec_seq), casts=[(w_up[l], (1, 2))])
    o_sa, w_val_bf = _attn_lat_a(qkvh_s, cache_a_k, cache_a_v, bias, w_up[l], (0, 2))
    o_sb, w_down_bf = _attn_lat_b(qkvh_s, cache_b_k, cache_b_v, w_down[l])
    x1_s, h2_s = _proj(o_sa, o_sb, xs, mod_lat, w_out_bf, g_post, g_fpre,
                       tiles_per_mod=dec_seq // 512)

    y_c = _ffn(h2_c, x1_c, mod_ctx, w_val_bf, w_gate_bf, conv_p, w_down_bf, g_fpost, seq=seq, tm=1024)
    y_s = _ffn(h2_s, x1_s, mod_lat, w_val_bf, w_gate_bf, conv_p, w_down_bf, g_fpost,
               seq=dec_seq, tm=1024)

    return (y_c.reshape(batch, seq, D_MODEL), y_s.reshape(dec_batch, dec_seq, D_MODEL),
            st_ak, st_av, st_bk, st_bv)
```

```python
import functools

import numpy as np
import jax
import jax.numpy as jnp
from jax import lax
from jax.experimental import pallas as pl
from jax.experimental.pallas import tpu as pltpu

F32 = jnp.float32
BF16 = jnp.bfloat16

D_MODEL = 2048
HEAD_DIM = 128
N_HEADS_A = 8
N_HEADS_B = 8
N_KV_B = 2
GRID_W = 64
GRID_H = 16
WIN_H = 8
WIN_W = 16
D_FF = 5632
ROPE_THETA = 10000.0
EPS = 1e-6
NEG_INF = -1e30
Q_SCALE = HEAD_DIM ** -0.5
IN_COLS = 4608
N_SLOTS = IN_COLS // HEAD_DIM
QKV_TN = 512
N_QKV_TILES = IN_COLS // QKV_TN
FFN_TF = 512
FFN_CW = 256
N_FFN_TILES = D_FF // FFN_TF
FIN_ROWS = 128
PROJ_ROWS = 128
ATT_ROWS = 128
HALF_Q = 512
BAND_K = 768
VMEM_LIMIT = 60 * 1024 * 1024


def _cparams(sem):
    return pltpu.CompilerParams(dimension_semantics=sem, vmem_limit_bytes=VMEM_LIMIT)


def _rms(x):
    return x * lax.rsqrt(jnp.mean(x * x, axis=-1, keepdims=True) + EPS)


def _mod_kernel(c_ref, w_ref, b_ref, o_ref):
    c = c_ref[...]
    a = (c / (1.0 + jnp.exp(-c))).astype(BF16)
    o_ref[...] = jnp.dot(a, w_ref[...].astype(BF16), preferred_element_type=F32) + b_ref[...]


def _modulation(conds, w_mod, b_mod):
    tn = 2048
    n = w_mod.shape[1]
    return pl.pallas_call(
        _mod_kernel,
        grid=(n // tn,),
        in_specs=[
            pl.BlockSpec((16, D_MODEL), lambda j: (0, 0)),
            pl.BlockSpec((D_MODEL, tn), lambda j: (0, j)),
            pl.BlockSpec((1, tn), lambda j: (0, j)),
        ],
        out_specs=pl.BlockSpec((16, tn), lambda j: (0, j)),
        out_shape=jax.ShapeDtypeStruct((16, n), F32),
        compiler_params=_cparams(("arbitrary",)),
        name="modulation",
    )(conds, w_mod, b_mod.reshape(1, n))


def _bias_kernel(rpb_ref, wf_ref, o_ref, wb_ref):
    _cast_slabs([wf_ref], [wb_ref])
    h = pl.program_id(0)
    qc = lax.broadcasted_iota(jnp.int32, (GRID_W, 128), 0)
    lane = lax.broadcasted_iota(jnp.int32, (GRID_W, 128), 1)
    kc = lane & (GRID_W - 1)
    dmat = kc - qc + (WIN_W - 1)
    cs = jnp.clip(qc - WIN_W // 2, 0, GRID_W - WIN_W)
    col_valid = (kc >= cs) & (kc < cs + WIN_W)
    neg = jnp.full((GRID_W, 128), NEG_INF, F32)

    g = []
    for dr in range(2 * WIN_H - 1):
        acc = neg
        for d in range(2 * WIN_W - 1):
            acc = jnp.where(dmat == d, rpb_ref[h, dr * (2 * WIN_W - 1) + d], acc)
        g.append(jnp.where(col_valid, acc, neg))

    def row_tile(r, kr):
        rs = min(max(r - WIN_H // 2, 0), GRID_H - WIN_H)
        if rs <= kr < rs + WIN_H:
            return g[kr - r + WIN_H - 1]
        return neg

    for half in range(2):
        for lr in range(HALF_Q // GRID_W):
            r = 8 * half + lr
            for p in range(BAND_K // 128):
                kr0 = 4 * half + 2 * p
                tile = jnp.where(lane < GRID_W, row_tile(r, kr0), row_tile(r, kr0 + 1))
                o_ref[0, half, lr * GRID_W:(lr + 1) * GRID_W, p * 128:(p + 1) * 128] = tile


def _bias_table(rpb_l, cast_w):
    rpb2 = rpb_l.reshape(N_HEADS_A, (2 * WIN_H - 1) * (2 * WIN_W - 1))
    w_in_spec, w_out_spec, w_shape = _cast_specs(cast_w, N_HEADS_A)
    return pl.pallas_call(
        _bias_kernel,
        grid=(N_HEADS_A,),
        in_specs=[pl.BlockSpec(memory_space=pltpu.SMEM), w_in_spec],
        out_specs=[pl.BlockSpec((1, 2, HALF_Q, BAND_K), lambda h: (h, 0, 0, 0)), w_out_spec],
        out_shape=[jax.ShapeDtypeStruct((N_HEADS_A, 2, HALF_Q, BAND_K), F32), w_shape],
        compiler_params=_cparams(("arbitrary",)),
        name="bias_table",
    )(rpb2, cast_w)


def _cast_specs(w, steps, *, col_block=(0, 1), grid_index=lambda i: i):
    k, n = col_block
    rows = w.shape[0] // steps
    cols = w.shape[1] // n
    assert rows * steps == w.shape[0] and rows % 16 == 0 and cols * n == w.shape[1]
    in_spec = pl.BlockSpec((rows, cols), lambda *g: (grid_index(*g), k))
    out_spec = pl.BlockSpec((rows, cols), lambda *g: (grid_index(*g), 0))
    return in_spec, out_spec, jax.ShapeDtypeStruct((w.shape[0], cols), BF16)


def _cast_slabs(cast_in, cast_out):
    for wf_ref, wb_ref in zip(cast_in, cast_out):
        wb_ref[...] = wf_ref[...].astype(BF16)


def _rope(y, cos, sin_signed):
    lane = lax.broadcasted_iota(jnp.int32, y.shape, 1)
    swapped = jnp.where((lane & 63) < 32, pltpu.roll(y, 96, 1), pltpu.roll(y, 32, 1))
    return y * cos + swapped * sin_signed


def _qkv_kernel(*refs, is_ctx, tm, seq, n_cast):
    n_in = (6 if is_ctx else 8) + n_cast
    n_out = 5 if is_ctx else 1
    _cast_slabs(refs[n_in - n_cast:n_in], refs[n_in + n_out:n_in + n_out + n_cast])
    if is_ctx:
        x_ref, mod_ref, gpre_ref, w_ref, gq_ref, gk_ref = refs[:6]
        o_ref, sak_ref, sav_ref, sbk_ref, sbv_ref = refs[n_in:n_in + 5]
        out_ref = refs[-1]
    else:
        x_ref, mod_ref, gpre_ref, w_ref, gq_ref, gk_ref, cos_ref, sin_ref = refs[:8]
        out_ref = refs[n_in]
    nb = max(tm // seq, 1)
    rows_per_state = min(seq, tm)
    heads_per_tile = QKV_TN // HEAD_DIM
    sh1 = mod_ref[0, 0:1, :]
    sc1 = mod_ref[0, 1:2, :]
    h = (_rms(x_ref[...]) * (gpre_ref[...] * (1.0 + sc1)) + sh1).astype(BF16)
    gq = gq_ref[...]
    gk = gk_ref[...]

    def project(n):
        return jnp.dot(h, w_ref[:, n * QKV_TN:(n + 1) * QKV_TN], preferred_element_type=F32)

    def store_state(s_ref, slot, val):
        for bb in range(nb):
            s_ref[bb, 0, slot] = val[bb * rows_per_state:(bb + 1) * rows_per_state]

    def rope(y):
        if is_ctx:
            return y
        return _rope(y, cos_ref[...], sin_ref[...])

    def epilogue(n, res):
        for hh in range(heads_per_tile):
            slot = n * heads_per_tile + hh
            v = res[:, hh * HEAD_DIM:(hh + 1) * HEAD_DIM]
            if slot < 8:
                out_ref[slot] = (v * Q_SCALE).astype(BF16)
            elif slot < 24:
                out_ref[slot] = v.astype(BF16)
                if is_ctx:
                    store_state(sak_ref if slot < 16 else sav_ref, slot % 8, v)
            elif slot < 32:
                out_ref[slot] = (rope(_rms(v) * gq) * Q_SCALE).astype(BF16)
            elif slot < 34:
                y = _rms(v) * gk
                if is_ctx:
                    store_state(sbk_ref, slot - 32, y)
                out_ref[slot] = rope(y).astype(BF16)
            else:
                out_ref[slot] = v.astype(BF16)
                if is_ctx:
                    store_state(sbv_ref, slot - 34, v)

    order = [6, 7, 8, 0, 1, 2, 3, 4, 5]
    res = project(order[0])
    for pos, n in enumerate(order):
        nxt = project(order[pos + 1]) if pos + 1 < len(order) else None
        epilogue(n, res)
        res = nxt
        if is_ctx and n == 8:
            _ctx_attn_b(out_ref, o_ref, nb=nb, seq=seq)
    if is_ctx:
        _ctx_attn_a(out_ref, o_ref, nb=nb, seq=seq)


def _qkv(x2d, mods, g_pre, w_in_bf, g_q, g_k, *, is_ctx, seq, rope_tabs=None, casts=()):
    t = x2d.shape[0]
    tm = 512
    nb = max(tm // seq, 1)
    tiles_per_seq = max(seq // tm, 1)
    nmod = mods.shape[0]
    mod_idx = (lambda i: (0, 0, 0)) if nmod == 1 else (lambda i: (i // tiles_per_seq, 0, 0))
    in_specs = [
        pl.BlockSpec((tm, D_MODEL), lambda i: (i, 0)),
        pl.BlockSpec((1, 6, D_MODEL), mod_idx),
        pl.BlockSpec((1, D_MODEL), lambda i: (0, 0)),
        pl.BlockSpec((D_MODEL, IN_COLS), lambda i: (0, 0), pipeline_mode=pl.Buffered(1)),
        pl.BlockSpec((1, HEAD_DIM), lambda i: (0, 0)),
        pl.BlockSpec((1, HEAD_DIM), lambda i: (0, 0)),
    ]
    args = [x2d, mods, g_pre, w_in_bf, g_q, g_k]
    out_specs = [pl.BlockSpec((N_SLOTS, tm, HEAD_DIM), lambda i: (0, i, 0))]
    out_shape = [jax.ShapeDtypeStruct((N_SLOTS, t, HEAD_DIM), BF16)]
    scratch = []
    if is_ctx:
        assert tm % seq == 0
        out_specs = [pl.BlockSpec((tm, D_MODEL), lambda i: (i, 0))]
        out_shape = [jax.ShapeDtypeStruct((t, D_MODEL), BF16)]
        scratch = [pltpu.VMEM((N_SLOTS, tm, HEAD_DIM), BF16)]
        nbatch = t // seq
        for nh in (N_HEADS_A, N_HEADS_A, N_KV_B, N_KV_B):
            out_specs.append(pl.BlockSpec((nb, 1, nh, seq, HEAD_DIM), lambda i: (i, 0, 0, 0, 0)))
            out_shape.append(jax.ShapeDtypeStruct((nbatch, 1, nh, seq, HEAD_DIM), F32))
    else:
        rope_idx = lambda i: (i % tiles_per_seq, 0)
        in_specs += [pl.BlockSpec((tm, HEAD_DIM), rope_idx), pl.BlockSpec((tm, HEAD_DIM), rope_idx)]
        args += list(rope_tabs)
    for w, col_block in casts:
        w_in_spec, w_out_spec, w_shape = _cast_specs(w, t // tm, col_block=col_block)
        in_specs.append(w_in_spec)
        args.append(w)
        out_specs.append(w_out_spec)
        out_shape.append(w_shape)
    return pl.pallas_call(
        functools.partial(_qkv_kernel, is_ctx=is_ctx, tm=tm, seq=seq, n_cast=len(casts)),
        grid=(t // tm,),
        in_specs=in_specs,
        out_specs=out_specs,
        out_shape=out_shape,
        scratch_shapes=scratch,
        compiler_params=_cparams(("arbitrary",)),
        name="qkv_ctx" if is_ctx else "qkv_lat",
    )(*args)


def _qk(q, k):
    return lax.dot_general(q, k, (((1,), (1,)), ((), ())), preferred_element_type=F32)


def _with_ones(v):
    return jnp.concatenate([v, jnp.ones_like(v)], axis=1)


def _softmax_pv(scores, values):
    m = functools.reduce(jnp.maximum, [jnp.max(s, axis=-1, keepdims=True) for s in scores])
    with_ones = values[0].shape[1] == 2 * HEAD_DIM
    acc = None
    den = None
    for s, v in zip(scores, values):
        e = jnp.exp(s - m)
        o = jnp.dot(e.astype(BF16), v, preferred_element_type=F32)
        acc = o if acc is None else acc + o
        if not with_ones:
            l = jnp.sum(e, axis=-1, keepdims=True)
            den = l if den is None else den + l
    if with_ones:
        return acc[:, :HEAD_DIM] / acc[:, HEAD_DIM:HEAD_DIM + 1]
    return acc / den


def _ctx_attn_a(qkv_ref, o_ref, *, nb, seq):
    for bb in range(nb):
        rows = slice(bb * seq, (bb + 1) * seq)
        for h in range(N_HEADS_A):
            q = qkv_ref[h, rows, :]
            k = qkv_ref[N_HEADS_A + h, rows, :]
            v = qkv_ref[2 * N_HEADS_A + h, rows, :]
            o = _softmax_pv([_qk(q, k)], [v])
            o_ref[rows, h * HEAD_DIM:(h + 1) * HEAD_DIM] = o.astype(BF16)


def _ctx_attn_b(qkv_ref, o_ref, *, nb, seq):
    group = N_HEADS_B // N_KV_B
    for bb in range(nb):
        rows = slice(bb * seq, (bb + 1) * seq)
        for j in range(N_KV_B):
            k = qkv_ref[32 + j, rows, :]
            v = qkv_ref[34 + j, rows, :]
            q = jnp.concatenate([qkv_ref[24 + j * group + g, rows, :] for g in range(group)], axis=0)
            o = _softmax_pv([_qk(q, k)], [v])
            for g in range(group):
                c0 = (N_HEADS_A + j * group + g) * HEAD_DIM
                o_ref[rows, c0:c0 + HEAD_DIM] = o[g * seq:(g + 1) * seq].astype(BF16)


def _attn_lat_a_kernel(q_ref, k0_ref, k1_ref, k2_ref, v0_ref, v1_ref, v2_ref,
                       ck_ref, cv_ref, bias_ref, wf_ref, o_ref, wb_ref):
    _cast_slabs([wf_ref], [wb_ref])
    for h in range(N_HEADS_A):
        ck = ck_ref[0, 0, h].astype(BF16)
        v_nb = _with_ones(jnp.concatenate([v0_ref[h], v1_ref[h], v2_ref[h]], axis=0))
        cv = _with_ones(cv_ref[0, 0, h].astype(BF16))
        for r in range(0, HALF_Q, ATT_ROWS):
            rows = slice(r, r + ATT_ROWS)
            q = q_ref[h, rows, :]
            s_nb = jnp.concatenate([_qk(q, k0_ref[h]), _qk(q, k1_ref[h]), _qk(q, k2_ref[h])], axis=1)
            s_nb = s_nb + bias_ref[h, 0, rows, :]
            o = _softmax_pv([s_nb, _qk(q, ck)], [v_nb, cv])
            o_ref[rows, h * HEAD_DIM:(h + 1) * HEAD_DIM] = o.astype(BF16)


def _attn_lat_b_kernel(q_ref, k_ref, v_ref, ck_ref, cv_ref, wf_ref, o_ref, wb_ref):
    _cast_slabs([wf_ref], [wb_ref])
    k = k_ref[0]
    v = _with_ones(v_ref[0])
    ck = ck_ref[0, 0, 0].astype(BF16)
    cv = _with_ones(cv_ref[0, 0, 0].astype(BF16))
    for g in range(N_HEADS_B // N_KV_B):
        for r in range(0, q_ref.shape[1], ATT_ROWS):
            rows = slice(r, r + ATT_ROWS)
            q = q_ref[g, rows, :]
            o = _softmax_pv([_qk(q, k), _qk(q, ck)], [v, cv])
            o_ref[rows, g * HEAD_DIM:(g + 1) * HEAD_DIM] = o.astype(BF16)


def _attn_lat_kernel(*refs):
    (q_ref, k0_ref, k1_ref, k2_ref, v0_ref, v1_ref, v2_ref, ck_ref, cv_ref, bias_ref, wfa_ref,
     qb_ref, kb_ref, vb_ref, ckb_ref, cvb_ref, wfb_ref, oa_ref, wba_ref, ob_ref, wbb_ref) = refs
    _attn_lat_a_kernel(q_ref, k0_ref, k1_ref, k2_ref, v0_ref, v1_ref, v2_ref, ck_ref, cv_ref, bias_ref,
                       wfa_ref, oa_ref, wba_ref)
    _attn_lat_b_kernel(qb_ref, kb_ref, vb_ref, ckb_ref, cvb_ref, wfb_ref, ob_ref, wbb_ref)


def _attn_lat(qkvh, cache_ak, cache_av, cache_bk, cache_bv, bias, cast_a, cast_a_cols, cast_b):
    t = qkvh.shape[1]
    nbatch = t // 1024
    kq = 256
    step = lambda hf, b: hf * nbatch + b
    wa_in, wa_out, wa_shape = _cast_specs(cast_a, 2 * nbatch, col_block=cast_a_cols, grid_index=step)
    wb_in, wb_out, wb_shape = _cast_specs(cast_b, 2 * nbatch, grid_index=step)

    def kspec(slot_blk, j):
        return pl.BlockSpec((8, kq, HEAD_DIM), lambda hf, b: (slot_blk, b * 4 + hf + j, 0))

    cache_a_spec = pl.BlockSpec((1, 1, N_HEADS_A, 256, HEAD_DIM), lambda hf, b: (b, 0, 0, 0, 0))
    cache_b_spec = pl.BlockSpec((1, 1, 1, 256, HEAD_DIM), lambda hf, b: (b, 0, hf, 0, 0))
    return pl.pallas_call(
        _attn_lat_kernel,
        grid=(2, nbatch),
        in_specs=[
            pl.BlockSpec((8, HALF_Q, HEAD_DIM), lambda hf, b: (0, b * 2 + hf, 0)),
            kspec(1, 0), kspec(1, 1), kspec(1, 2),
            kspec(2, 0), kspec(2, 1), kspec(2, 2),
            cache_a_spec, cache_a_spec,
            pl.BlockSpec((N_HEADS_A, 1, HALF_Q, BAND_K), lambda hf, b: (0, hf, 0, 0),
                         pipeline_mode=pl.Buffered(1)),
            wa_in,
            pl.BlockSpec((4, 1024, HEAD_DIM), lambda hf, b: (6 + hf, b, 0)),
            pl.BlockSpec((1, 1024, HEAD_DIM), lambda hf, b: (32 + hf, b, 0)),
            pl.BlockSpec((1, 1024, HEAD_DIM), lambda hf, b: (34 + hf, b, 0)),
            cache_b_spec, cache_b_spec, wb_in,
        ],
        out_specs=[pl.BlockSpec((HALF_Q, N_HEADS_A * HEAD_DIM), lambda hf, b: (b * 2 + hf, 0)), wa_out,
                   pl.BlockSpec((1024, 512), lambda hf, b: (b, hf)), wb_out],
        out_shape=[jax.ShapeDtypeStruct((t, N_HEADS_A * HEAD_DIM), BF16), wa_shape,
                   jax.ShapeDtypeStruct((t, N_HEADS_B * HEAD_DIM), BF16), wb_shape],
        compiler_params=_cparams(("arbitrary", "arbitrary")),
        name="attn_lat",
    )(qkvh, qkvh, qkvh, qkvh, qkvh, qkvh, qkvh, cache_ak, cache_av, bias, cast_a,
      qkvh, qkvh, qkvh, cache_bk, cache_bv, cast_b)


def _proj_kernel(oa_ref, ob_ref, x_ref, mod_ref, w_ref, gpost_ref, gffn_ref, x1_ref, h2_ref):
    gain1 = mod_ref[0, 2:3, :] * gpost_ref[...]
    sh2 = mod_ref[0, 3:4, :]
    gain2 = gffn_ref[...] * (1.0 + mod_ref[0, 4:5, :])
    tm, ka = oa_ref.shape
    chunks = [slice(r, r + PROJ_ROWS) for r in range(0, tm, PROJ_ROWS)]

    def project(rows):
        return (jnp.dot(oa_ref[rows, :], w_ref[:ka, :], preferred_element_type=F32)
                + jnp.dot(ob_ref[rows, :], w_ref[ka:, :], preferred_element_type=F32))

    t = project(chunks[0])
    for r, rows in enumerate(chunks):
        nxt = project(chunks[r + 1]) if r + 1 < len(chunks) else None
        x1 = x_ref[rows, :] + gain1 * _rms(t)
        x1_ref[rows, :] = x1
        h2_ref[rows, :] = (_rms(x1) * gain2 + sh2).astype(BF16)
        t = nxt


def _proj(o_a, o_b, x2d, mods, w_out_bf, g_post, g_ffn, *, tiles_per_mod):
    t = x2d.shape[0]
    tm = 512
    nmod = mods.shape[0]
    mod_idx = (lambda i: (0, 0, 0)) if nmod == 1 else (lambda i: (i // tiles_per_mod, 0, 0))
    ka = D_MODEL // 2
    ob_idx = (lambda i: (i, 1)) if o_b.shape[1] == D_MODEL else (lambda i: (i, 0))
    in_specs = [
        pl.BlockSpec((tm, ka), lambda i: (i, 0)),
        pl.BlockSpec((tm, ka), ob_idx),
        pl.BlockSpec((tm, D_MODEL), lambda i: (i, 0)),
        pl.BlockSpec((1, 6, D_MODEL), mod_idx),
        pl.BlockSpec((D_MODEL, D_MODEL), lambda i: (0, 0), pipeline_mode=pl.Buffered(1)),
        pl.BlockSpec((1, D_MODEL), lambda i: (0, 0)),
        pl.BlockSpec((1, D_MODEL), lambda i: (0, 0)),
    ]
    out_specs = [
        pl.BlockSpec((tm, D_MODEL), lambda i: (i, 0)),
        pl.BlockSpec((tm, D_MODEL), lambda i: (i, 0)),
    ]
    out_shape = [
        jax.ShapeDtypeStruct((t, D_MODEL), F32),
        jax.ShapeDtypeStruct((t, D_MODEL), BF16),
    ]
    return pl.pallas_call(
        _proj_kernel,
        grid=(t // tm,),
        in_specs=in_specs,
        out_specs=out_specs,
        out_shape=out_shape,
        compiler_params=_cparams(("arbitrary",)),
        name="proj",
    )(o_a, o_b, x2d, mods, w_out_bf, g_post, g_ffn)


def _ffn_kernel(h2_ref, x1_ref, mod_ref, wv_ref, wg_ref, cpv_ref, cpg_ref,
                wd_ref, gpost_ref, out_ref, x1_buf, x1_sem, *u_scr, tm, seq):
    i = pl.program_id(0)
    j = pl.program_id(1)
    nseq = tm // seq
    stride = seq + 8
    chunks = [slice(c * FFN_CW, (c + 1) * FFN_CW) for c in range(FFN_TF // FFN_CW)]

    def x1_copy():
        return pltpu.make_async_copy(x1_ref.at[pl.ds(pl.multiple_of(i * tm, tm), tm), :], x1_buf, x1_sem)

    def conv(scr, base, cp_ref):
        lo = scr[base - 1:base - 1 + seq, :]
        mid = scr[base:base + seq, :]
        hi = scr[base + 1:base + 1 + seq, :]
        return (mid * cp_ref[1:2, :] + cp_ref[3:4, :]) + lo * cp_ref[0:1, :] + hi * cp_ref[2:3, :]

    def step(first):
        h2 = h2_ref[...]
        for c, cols in enumerate(chunks):
            for k, w_ref in enumerate((wv_ref, wg_ref)):
                u = jnp.dot(h2, w_ref[:, cols], preferred_element_type=F32)
                for s in range(nseq):
                    u_scr[2 * c + k][8 + s * stride:8 + s * stride + seq, :] = u[s * seq:(s + 1) * seq]
        for c, cols in enumerate(chunks):
            acts = []
            for s in range(nseq):
                base = 8 + s * stride
                val = conv(u_scr[2 * c], base, cpv_ref.at[:, cols])
                gate = conv(u_scr[2 * c + 1], base, cpg_ref.at[:, cols])
                acts.append((gate / (1.0 + jnp.exp(-gate)) * val).astype(BF16))
            act = acts[0] if nseq == 1 else jnp.concatenate(acts, axis=0)
            d = jnp.dot(act, wd_ref[cols, :], preferred_element_type=F32)
            if first and c == 0:
                out_ref[...] = d
            else:
                out_ref[...] += d

    @pl.when(j == 0)
    def _():
        x1_copy().start()
        for scr in u_scr:
            for s in range(nseq + 1):
                scr[s * stride:s * stride + 8, :] = jnp.zeros((8, FFN_CW), F32)
        step(True)

    @pl.when(j > 0)
    def _():
        step(False)

    @pl.when(j == N_FFN_TILES - 1)
    def _():
        x1_copy().wait()
        gain = mod_ref[0, 5:6, :] * gpost_ref[...]
        for r in range(0, tm, FIN_ROWS):
            rows = slice(r, r + FIN_ROWS)
            out_ref[rows, :] = x1_buf[rows, :] + gain * _rms(out_ref[rows, :])


def _ffn(h2, x1, mods, w_val_bf, w_gate_bf, conv_p, w_down_bf, g_post, *, seq, tm):
    t = x1.shape[0]
    nmod = mods.shape[0]
    mod_idx = (lambda i, j: (0, 0, 0)) if nmod == 1 else (lambda i, j: (i * tm // seq, 0, 0))
    nj = N_FFN_TILES
    u_rows = 8 + (tm // seq) * (seq + 8)
    return pl.pallas_call(
        functools.partial(_ffn_kernel, tm=tm, seq=seq),
        grid=(t // tm, nj),
        in_specs=[
            pl.BlockSpec((tm, D_MODEL), lambda i, j: (i, 0)),
            pl.BlockSpec(memory_space=pl.ANY),
            pl.BlockSpec((1, 6, D_MODEL), mod_idx),
            pl.BlockSpec((D_MODEL, FFN_TF), lambda i, j: (0, j)),
            pl.BlockSpec((D_MODEL, FFN_TF), lambda i, j: (0, j)),
            pl.BlockSpec((4, FFN_TF), lambda i, j: (0, j)),
            pl.BlockSpec((4, FFN_TF), lambda i, j: (0, nj + j)),
            pl.BlockSpec((FFN_TF, D_MODEL), lambda i, j: (j, 0)),
            pl.BlockSpec((1, D_MODEL), lambda i, j: (0, 0)),
        ],
        out_specs=pl.BlockSpec((tm, D_MODEL), lambda i, j: (i, 0)),
        out_shape=jax.ShapeDtypeStruct((t, D_MODEL), F32),
        scratch_shapes=[pltpu.VMEM((tm, D_MODEL), F32), pltpu.SemaphoreType.DMA(())]
        + [pltpu.VMEM((u_rows, FFN_CW), F32) for _ in range(2 * (FFN_TF // FFN_CW))],
        compiler_params=_cparams(("arbitrary", "arbitrary")),
        name="ffn",
    )(h2, x1, mods, w_val_bf, w_gate_bf, conv_p, conv_p, w_down_bf, g_post)


def _rope_tables(n_tok):
    half = HEAD_DIM // 2
    t = np.arange(n_tok)
    freqs = ROPE_THETA ** (-np.arange(0, half, 2, dtype=np.float64) / half)
    lane = np.arange(HEAD_DIM)
    pos = np.where(lane[None, :] < half, (t // GRID_W)[:, None], (t % GRID_W)[:, None])
    ang = pos * freqs[lane % (half // 2)][None, :]
    sign = np.where((lane % half) < half // 2, -1.0, 1.0)[None, :]
    return (jnp.asarray(np.cos(ang), F32), jnp.asarray(np.sin(ang) * sign, F32))


def kernel(x_prompt, x_sample, c, cache_a_k, cache_a_v, cache_b_k, cache_b_v, c_ctx, w_mod, b_mod,
           g_attn_pre, g_attn_post, g_ffn_pre, g_ffn_post, w_in, rpb, g_qnorm, g_knorm, w_out,
           w_up, conv_w, conv_b, w_down):
    batch, seq, _ = x_prompt.shape
    dec_batch, dec_seq, _ = x_sample.shape
    depth = w_mod.shape[0]
    assert depth == 1 and dec_seq == GRID_H * GRID_W

    xp = x_prompt.reshape(batch * seq, D_MODEL)
    xs = x_sample.reshape(dec_batch * dec_seq, D_MODEL)
    l = 0
    conds = jnp.concatenate(
        [c_ctx[None, :], c, jnp.zeros((16 - 1 - dec_batch, D_MODEL), F32)], axis=0)
    mods = _modulation(conds, w_mod[l], b_mod[l]).reshape(16, 6, D_MODEL)
    mod_ctx = mods[0:1]
    mod_lat = mods[1:1 + dec_batch]

    g_pre = g_attn_pre[l].reshape(1, D_MODEL)
    g_post = g_attn_post[l].reshape(1, D_MODEL)
    g_fpre = g_ffn_pre[l].reshape(1, D_MODEL)
    g_fpost = g_ffn_post[l].reshape(1, D_MODEL)
    g_q = g_qnorm[l].reshape(1, HEAD_DIM)
    g_k = g_knorm[l].reshape(1, HEAD_DIM)
    conv_p = jnp.concatenate([conv_w[l], conv_b[l].reshape(1, 2 * D_FF)], axis=0)

    bias, w_in_bf = _bias_table(rpb[l], w_in[l])

    o_c, st_ak, st_av, st_bk, st_bv, w_out_bf = _qkv(
        xp, mod_ctx, g_pre, w_in_bf, g_q, g_k, is_ctx=True, seq=seq, casts=[(w_out[l], (0, 1))])
    x1_c, h2_c = _proj(o_c, o_c, xp, mod_ctx, w_out_bf, g_post, g_fpre, tiles_per_mod=1)

    qkvh_s, w_gate_bf = _qkv(xs, mod_lat, g_pre, w_in_bf, g_q, g_k, is_ctx=False, seq=dec_seq,
                             rope_tabs=_rope_tables(dec_seq), casts=[(w_up[l], (1, 2))])
    o_sa, w_val_bf, o_sb, w_down_bf = _attn_lat(qkvh_s, cache_a_k, cache_a_v, cache_b_k, cache_b_v, bias,
                                                w_up[l], (0, 2), w_down[l])
    x1_s, h2_s = _proj(o_sa, o_sb, xs, mod_lat, w_out_bf, g_post, g_fpre,
                       tiles_per_mod=dec_seq // 512)

    y_c = _ffn(h2_c, x1_c, mod_ctx, w_val_bf, w_gate_bf, conv_p, w_down_bf, g_fpost, seq=seq, tm=1024)
    y_s = _ffn(h2_s, x1_s, mod_lat, w_val_bf, w_gate_bf, conv_p, w_down_bf, g_fpost,
               seq=dec_seq, tm=1024)

    return (y_c.reshape(batch, seq, D_MODEL), y_s.reshape(dec_batch, dec_seq, D_MODEL),
            st_ak, st_av, st_bk, st_bv)
```
